```python
import math
import jax, jax.numpy as jnp
from jax import lax
import numpy as np

D_MODEL = 1024
BATCH = 8
SEQ = 2048
DEPTH = 2

MEM_LEN = 256
MLA_HEADS = 8
MLA_NOPE_DIM = 64
MLA_ROPE_DIM = 32
MLA_V_DIM = 64
Q_LORA_RANK = 384
KV_LORA_RANK = 256
ROPE_THETA = 10000.0
DIL_HEADS = 8
DIL_HEAD_DIM = 64
DIL_PATTERNS = ((128, 1), (512, 4), (2048, 16))
N_BUCKETS = 32
MAX_DISTANCE = 2048
X_HEADS = 4
X_HEAD_DIM = D_MODEL // X_HEADS
D_FF = 4 * D_MODEL

Q_BLOCK = 128
EPS = 1e-6
MLA_WIDTH = MLA_HEADS * MLA_V_DIM
DIL_WIDTH = DIL_HEADS * DIL_HEAD_DIM
MIX_WIDTH = MLA_WIDTH + DIL_WIDTH
MLA_QK_DIM = MLA_NOPE_DIM + MLA_ROPE_DIM
IN_WIDTH = Q_LORA_RANK + KV_LORA_RANK + MLA_ROPE_DIM + 3 * DIL_WIDTH

kernel_name = "hymba_mla_dilated_hybrid_block"


def rms_norm(x, g):
    xf = x.astype(jnp.float32)
    y = xf * lax.rsqrt(jnp.mean(xf * xf, axis=-1, keepdims=True) + EPS)
    return (y * g.astype(jnp.float32)).astype(x.dtype)


def rope(x, positions):
    half = x.shape[-1] // 2
    inv = ROPE_THETA ** (-jnp.arange(half, dtype=jnp.float32) / half)
    ang = positions.astype(jnp.float32)[..., None] * inv
    cos = jnp.cos(ang)[:, :, None, :]
    sin = jnp.sin(ang)[:, :, None, :]
    x1 = x[..., :half].astype(jnp.float32)
    x2 = x[..., half:].astype(jnp.float32)
    out = jnp.concatenate([x1 * cos - x2 * sin, x2 * cos + x1 * sin], axis=-1)
    return out.astype(x.dtype)


def t5_bucket(dist):
    max_exact = N_BUCKETS // 2
    d = jnp.maximum(dist, 0)
    df = jnp.maximum(d.astype(jnp.float32), 1.0)
    large = max_exact + (jnp.log(df / max_exact) / math.log(MAX_DISTANCE / max_exact)
                         * (N_BUCKETS - max_exact)).astype(jnp.int32)
    large = jnp.minimum(large, N_BUCKETS - 1)
    return jnp.where(d < max_exact, d, large)


def causal_block_attention(q, k, v):
    B, S, H, Dqk = q.shape
    scale = Dqk ** -0.5
    nblk = S // Q_BLOCK
    qb = q.reshape(B, nblk, Q_BLOCK, H, Dqk).transpose(1, 0, 2, 3, 4)
    kpos = jnp.arange(S)

    def block(args):
        qi, bi = args
        s = jnp.einsum('bqhd,bkhd->bhqk', qi, k, preferred_element_type=jnp.float32) * scale
        qpos = bi * Q_BLOCK + jnp.arange(Q_BLOCK)
        s = jnp.where(kpos[None, :] <= qpos[:, None], s, -jnp.inf)
        p = jax.nn.softmax(s, axis=-1).astype(v.dtype)
        return jnp.einsum('bhqk,bkhd->bqhd', p, v)

    out = lax.map(block, (qb, jnp.arange(nblk)))
    return out.transpose(1, 0, 2, 3, 4).reshape(B, S, H, v.shape[-1])


def dilated_window_attention(q, k, v, rel_bias, window, dil):
    B, S, H, D = q.shape
    n = window // dil
    span = dil * n
    Sp = -(-S // span) * span
    L = Sp // dil
    nc = L // n
    scale = D ** -0.5

    def to_strided(t):
        t = jnp.pad(t, ((0, 0), (0, Sp - S), (0, 0), (0, 0)))
        t = t.reshape(B, L, dil, H, D)
        t = t.transpose(0, 2, 1, 3, 4)
        return t.reshape(B, dil, nc, n, H, D)

    qq, kc, vc = to_strided(q), to_strided(k), to_strided(v)

    def with_prev(t):
        prev = jnp.concatenate([jnp.zeros_like(t[:, :, :1]), t[:, :, :-1]], axis=2)
        return jnp.concatenate([prev, t], axis=3)

    kk, vv = with_prev(kc), with_prev(vc)

    i = jnp.arange(n)[:, None]
    m = jnp.arange(2 * n)[None, :]
    rel = n + i - m
    band = (rel >= 0) & (rel <= n)
    c = jnp.arange(nc)[:, None, None]
    mask = (band[None] & ((c > 0) | (m[None] >= n)))[:, None]
    bias = rel_bias.astype(jnp.float32)[t5_bucket(rel * dil)].transpose(2, 0, 1)

    s = jnp.einsum('brcqhd,brckhd->brchqk', qq, kk, preferred_element_type=jnp.float32) * scale + bias
    s = jnp.where(mask, s, -jnp.inf)
    s_max = jnp.max(s, axis=-1, keepdims=True)
    e = jnp.exp(s - s_max)
    den = jnp.sum(e, axis=-1)
    o = jnp.einsum('brchqk,brckhd->brcqhd', e, vv.astype(jnp.float32))
    den_q = den.transpose(0, 1, 2, 4, 3)
    o = o / den_q[..., None]
    lse = s_max[..., 0].transpose(0, 1, 2, 4, 3) + jnp.log(den_q)

    def from_strided(t):
        t = t.reshape((B, dil, L) + t.shape[4:])
        t = jnp.moveaxis(t, 1, 2)
        return t.reshape((B, Sp) + t.shape[3:])[:, :S]

    return from_strided(o), from_strided(lse)


def hybrid_mixer(h, positions, rel_bias, w_in, q_norm_g, w_uq, kv_norm_g, w_ukv, w_out):
    B, S, _ = h.shape
    z = h @ w_in
    o1 = Q_LORA_RANK
    o2 = o1 + KV_LORA_RANK
    o3 = o2 + MLA_ROPE_DIM
    o4 = o3 + DIL_WIDTH
    o5 = o4 + DIL_WIDTH
    c_q, c_kv, k_rope = z[..., :o1], z[..., o1:o2], z[..., o2:o3]
    q_d, k_d, v_d = z[..., o3:o4], z[..., o4:o5], z[..., o5:]

    q = (rms_norm(c_q, q_norm_g) @ w_uq).reshape(B, S, MLA_HEADS, MLA_QK_DIM)
    q = jnp.concatenate([q[..., :MLA_NOPE_DIM], rope(q[..., MLA_NOPE_DIM:], positions)], axis=-1)
    kv = (rms_norm(c_kv, kv_norm_g) @ w_ukv).reshape(B, S, MLA_HEADS, MLA_NOPE_DIM + MLA_V_DIM)
    k_nope, v_mla = kv[..., :MLA_NOPE_DIM], kv[..., MLA_NOPE_DIM:]
    k_pe = rope(k_rope[:, :, None, :], positions)
    k_pe = jnp.broadcast_to(k_pe, (B, S, MLA_HEADS, MLA_ROPE_DIM))
    k = jnp.concatenate([k_nope, k_pe], axis=-1)
    o_mla = causal_block_attention(q, k, v_mla).reshape(B, S, MLA_WIDTH)

    qd = q_d.reshape(B, S, DIL_HEADS, DIL_HEAD_DIM)
    kd = k_d.reshape(B, S, DIL_HEADS, DIL_HEAD_DIM)
    vd = v_d.reshape(B, S, DIL_HEADS, DIL_HEAD_DIM)
    outs, lses = [], []
    for window, dil in DIL_PATTERNS:
        o_p, lse_p = dilated_window_attention(qd, kd, vd, rel_bias, window, dil)
        outs.append(o_p)
        lses.append(lse_p)
    w = jax.nn.softmax(jnp.stack(lses, axis=0), axis=0)
    o_dil = jnp.sum(w[..., None] * jnp.stack(outs, axis=0), axis=0)
    o_dil = o_dil.astype(h.dtype).reshape(B, S, DIL_WIDTH)

    return jnp.concatenate([o_mla, o_dil], axis=-1) @ w_out


def memory_cross_attention(h, mem_n, w_xq, w_xk, w_xv, w_xo):
    B, S, _ = h.shape
    M = mem_n.shape[1]
    q = (h @ w_xq).reshape(B, S, X_HEADS, X_HEAD_DIM)
    k = (mem_n @ w_xk).reshape(B, M, X_HEADS, X_HEAD_DIM)
    v = (mem_n @ w_xv).reshape(B, M, X_HEADS, X_HEAD_DIM)
    s = jnp.einsum('bshd,bmhd->bhsm', q, k, preferred_element_type=jnp.float32) * (X_HEAD_DIM ** -0.5)
    p = jax.nn.softmax(s, axis=-1).astype(v.dtype)
    o = jnp.einsum('bhsm,bmhd->bshd', p, v).reshape(B, S, D_MODEL)
    return o @ w_xo


def setup_inputs(seed: int = 0) -> dict:
    key = jax.random.key(seed)
    ks = jax.random.split(key, 32)

    def w(k, shape, fan_in):
        return jax.random.normal(k, shape, jnp.float32) * fan_in ** -0.5

    def gain(k, dim):
        return 1.0 + 0.05 * jax.random.normal(k, (DEPTH, dim), jnp.float32)

    offset = jax.random.randint(ks[2], (BATCH, 1), 0, 1024, dtype=jnp.int32)
    positions = offset + jnp.arange(SEQ, dtype=jnp.int32)[None, :]
    return {
        "x": jax.random.normal(ks[0], (BATCH, SEQ, D_MODEL), jnp.float32),
        "mem": jax.random.normal(ks[1], (BATCH, MEM_LEN, D_MODEL), jnp.float32),
        "positions": positions,
        "rel_bias": 0.5 * jax.random.normal(ks[3], (N_BUCKETS, DIL_HEADS), jnp.float32),
        "w_in": w(ks[4], (DEPTH, D_MODEL, IN_WIDTH), D_MODEL),
        "q_norm_g": gain(ks[5], Q_LORA_RANK),
        "w_uq": w(ks[6], (DEPTH, Q_LORA_RANK, MLA_HEADS * MLA_QK_DIM), Q_LORA_RANK),
        "kv_norm_g": gain(ks[7], KV_LORA_RANK),
        "w_ukv": w(ks[8], (DEPTH, KV_LORA_RANK, MLA_HEADS * (MLA_NOPE_DIM + MLA_V_DIM)), KV_LORA_RANK),
        "w_mix_out": w(ks[9], (DEPTH, MIX_WIDTH, D_MODEL), MIX_WIDTH),
        "pre_mix_g": gain(ks[10], D_MODEL),
        "post_mix_g": gain(ks[11], D_MODEL),
        "w_xq": w(ks[12], (DEPTH, D_MODEL, D_MODEL), D_MODEL),
        "w_xk": w(ks[13], (DEPTH, D_MODEL, D_MODEL), D_MODEL),
        "w_xv": w(ks[14], (DEPTH, D_MODEL, D_MODEL), D_MODEL),
        "w_xo": w(ks[15], (DEPTH, D_MODEL, D_MODEL), D_MODEL),
        "mem_g": gain(ks[16], D_MODEL),
        "pre_xattn_g": gain(ks[17], D_MODEL),
        "post_xattn_g": gain(ks[18], D_MODEL),
        "w_up": w(ks[19], (DEPTH, D_MODEL, D_FF), D_MODEL),
        "w_down": w(ks[20], (DEPTH, D_FF, D_MODEL), D_FF),
        "pre_mlp_g": gain(ks[21], D_MODEL),
        "post_mlp_g": gain(ks[22], D_MODEL),
    }


def reference(x, mem, positions, rel_bias, w_in, q_norm_g, w_uq, kv_norm_g, w_ukv, w_mix_out,
              pre_mix_g, post_mix_g, w_xq, w_xk, w_xv, w_xo, mem_g, pre_xattn_g, post_xattn_g,
              w_up, w_down, pre_mlp_g, post_mlp_g):
    for l in range(DEPTH):
        a = hybrid_mixer(rms_norm(x, pre_mix_g[l]), positions, rel_bias, w_in[l], q_norm_g[l],
                         w_uq[l], kv_norm_g[l], w_ukv[l], w_mix_out[l])
        x = x + rms_norm(a, post_mix_g[l])
        c = memory_cross_attention(rms_norm(x, pre_xattn_g[l]), rms_norm(mem, mem_g[l]),
                                   w_xq[l], w_xk[l], w_xv[l], w_xo[l])
        x = x + rms_norm(c, post_xattn_g[l])
        hdn = jnp.square(jax.nn.relu(rms_norm(x, pre_mlp_g[l]) @ w_up[l]))
        x = x + rms_norm(hdn @ w_down[l], post_mlp_g[l])
    return x
```

```python
import functools
import math

import numpy as np
import jax
import jax.numpy as jnp
from jax import lax
from jax.experimental import pallas as pl
from jax.experimental.pallas import tpu as pltpu

D_MODEL = 1024
DEPTH = 2
MEM_LEN = 256
MLA_HEADS = 8
MLA_NOPE_DIM = 64
MLA_ROPE_DIM = 32
MLA_V_DIM = 64
Q_LORA_RANK = 384
KV_LORA_RANK = 256
ROPE_THETA = 10000.0
DIL_HEADS = 8
DIL_HEAD_DIM = 64
DIL_PATTERNS = ((128, 1), (512, 4), (2048, 16))
N_BUCKETS = 32
MAX_DISTANCE = 2048
X_HEADS = 4
X_HEAD_DIM = D_MODEL // X_HEADS
D_FF = 4 * D_MODEL
EPS = 1e-6

MLA_QK_DIM = MLA_NOPE_DIM + MLA_ROPE_DIM
DIL_WIDTH = DIL_HEADS * DIL_HEAD_DIM
MLA_WIDTH = MLA_HEADS * MLA_V_DIM
ROPE_HALF = MLA_ROPE_DIM // 2
DIL_N = 128
LANES = 128
NEG_BIG = -1e30
VMEM_LIMIT_BYTES = 56 * 1024 * 1024

BF16 = jnp.bfloat16
F32 = jnp.float32


def _rms(x, g):
    return x * lax.rsqrt(jnp.mean(x * x, axis=-1, keepdims=True) + EPS) * g


def _dot(a, b):
    return jnp.dot(a, b, preferred_element_type=F32)


def _dot_nt(a, b):
    return lax.dot_general(a, b, (((1,), (1,)), ((), ())), preferred_element_type=F32)


def _params(*sem):
    return pltpu.CompilerParams(dimension_semantics=sem, vmem_limit_bytes=VMEM_LIMIT_BYTES)


def _rope_table_kernel(pos_ref, inv_ref, cos_ref, sin_ref):
    ang = pos_ref[...] * inv_ref[...]
    cos_ref[...] = jnp.cos(ang)
    sin_ref[...] = jnp.sin(ang)


def _rope_tables(positions):
    n = positions.size
    rows = n // 8
    pos_rep = jnp.repeat(positions.reshape(rows, 8).astype(F32), ROPE_HALF, axis=1)
    inv = ROPE_THETA ** (-jnp.arange(ROPE_HALF, dtype=F32) / ROPE_HALF)
    inv_rep = jnp.tile(inv, 8).reshape(1, LANES)
    tr = 256
    cos, sin = pl.pallas_call(
        _rope_table_kernel,
        grid=(rows // tr,),
        in_specs=[pl.BlockSpec((tr, LANES), lambda i: (i, 0)),
                  pl.BlockSpec((1, LANES), lambda i: (0, 0))],
        out_specs=[pl.BlockSpec((tr, LANES), lambda i: (i, 0))] * 2,
        out_shape=[jax.ShapeDtypeStruct((rows, LANES), F32)] * 2,
        compiler_params=_params("parallel"),
        name="rope_tables",
    )(pos_rep, inv_rep)
    cos = cos.reshape(n, ROPE_HALF)
    sin = sin.reshape(n, ROPE_HALF)
    ct = jnp.concatenate([jnp.ones((n, MLA_NOPE_DIM), F32), cos, cos, jnp.zeros((n, 32), F32)], axis=1)
    st = jnp.concatenate([jnp.zeros((n, MLA_NOPE_DIM), F32), sin, sin, jnp.zeros((n, 32), F32)], axis=1)
    return ct, st


def _bucket_tables():
    i = np.arange(DIL_N)[:, None]
    m = np.arange(2 * DIL_N)[None, :]
    rel = DIL_N + i - m
    band = (rel >= 0) & (rel <= DIL_N)
    max_exact = N_BUCKETS // 2
    out = []
    for _, dil in DIL_PATTERNS:
        d = np.maximum(rel * dil, 0)
        df = np.maximum(d.astype(np.float64), 1.0)
        large = max_exact + (np.log(df / max_exact) / math.log(MAX_DISTANCE / max_exact)
                             * (N_BUCKETS - max_exact)).astype(np.int32)
        large = np.minimum(large, N_BUCKETS - 1)
        bucket = np.where(d < max_exact, d, large)
        out.append(np.where(band, bucket, -1).astype(np.int32))
    return np.stack(out)


def _bias_table_kernel(rb_ref, bucket_ref, out_ref):
    h = pl.program_id(1)
    bucket = bucket_ref[0]
    tab = jnp.full(bucket.shape, NEG_BIG, F32)
    for b in range(N_BUCKETS):
        tab = jnp.where(bucket == b, rb_ref[b, h], tab)
    col = lax.broadcasted_iota(jnp.int32, bucket.shape, 1)
    out_ref[0, 0, 0] = jnp.where(col >= DIL_N, tab, NEG_BIG)
    out_ref[0, 1, 0] = tab


def _bias_tables(rel_bias):
    buckets = jnp.asarray(_bucket_tables())
    npat = len(DIL_PATTERNS)
    return pl.pallas_call(
        _bias_table_kernel,
        grid=(npat, DIL_HEADS),
        in_specs=[pl.BlockSpec(memory_space=pltpu.SMEM),
                  pl.BlockSpec((1, DIL_N, 2 * DIL_N), lambda p, h: (p, 0, 0))],
        out_specs=pl.BlockSpec((1, 2, 1, DIL_N, 2 * DIL_N), lambda p, h: (p, 0, h, 0, 0)),
        out_shape=jax.ShapeDtypeStruct((npat, 2, DIL_HEADS, DIL_N, 2 * DIL_N), F32),
        compiler_params=_params("parallel", "parallel"),
        name="bias_tables",
    )(rel_bias.astype(F32), buckets)


C_KRA = Q_LORA_RANK + KV_LORA_RANK
C_KRB = C_KRA + LANES
C_QD = C_KRB + LANES
C_KD = C_QD + DIL_WIDTH
C_VD = C_KD + DIL_WIDTH
IN_EXT = C_VD + DIL_WIDTH
QPAD = MLA_HEADS * LANES


def _mixer_in_kernel(x_ref, gpre_ref, win_ref, gq_ref, wuq_ref, gkv_ref, wukv_ref, ct_ref, st_ref,
                     q_ref, k_ref, v_ref, qd_ref, kd_ref, vd_ref):
    xn = _rms(x_ref[...], gpre_ref[...])
    z = _dot(xn.astype(BF16), win_ref[...])
    ct = ct_ref[...]
    st = st_ref[...]

    qd_ref[...] = (z[:, C_QD:C_KD] * (DIL_HEAD_DIM ** -0.5)).astype(BF16)
    kd_ref[...] = z[:, C_KD:C_VD].astype(BF16)
    vd_ref[...] = z[:, C_VD:IN_EXT].astype(BF16)

    cqn = _rms(z[:, :Q_LORA_RANK], gq_ref[...])
    q2 = _dot(cqn.astype(BF16), wuq_ref[...])
    scale = MLA_QK_DIM ** -0.5
    for h in range(MLA_HEADS):
        a = q2[:, h * LANES:(h + 1) * LANES]
        b = q2[:, QPAD + h * LANES:QPAD + (h + 1) * LANES]
        q_ref[:, h * LANES:(h + 1) * LANES] = ((a * ct + b * st) * scale).astype(BF16)

    ckvn = _rms(z[:, Q_LORA_RANK:C_KRA], gkv_ref[...])
    kv = _dot(ckvn.astype(BF16), wukv_ref[...])
    v_ref[...] = kv[:, QPAD:].astype(BF16)
    kpe = z[:, C_KRA:C_KRB] * ct + z[:, C_KRB:C_QD] * st
    for h in range(MLA_HEADS):
        k_ref[:, h * LANES:(h + 1) * LANES] = (kv[:, h * LANES:(h + 1) * LANES] + kpe).astype(BF16)


def _prep_mixer_weights(w_in, w_uq, w_ukv):
    o1, o2 = Q_LORA_RANK, Q_LORA_RANK + KV_LORA_RANK
    o3 = o2 + MLA_ROPE_DIM
    k1 = w_in[:, o2:o2 + ROPE_HALF]
    k2 = w_in[:, o2 + ROPE_HALF:o3]
    zc = lambda n: jnp.zeros((w_in.shape[0], n), w_in.dtype)
    kra = jnp.concatenate([zc(MLA_NOPE_DIM), k1, k2, zc(32)], axis=1)
    krb = jnp.concatenate([zc(MLA_NOPE_DIM), -k2, k1, zc(32)], axis=1)
    win = jnp.concatenate([w_in[:, :o2], kra, krb, w_in[:, o3:]], axis=1).astype(BF16)

    wq = w_uq.reshape(Q_LORA_RANK, MLA_HEADS, MLA_QK_DIM)
    zq = jnp.zeros((Q_LORA_RANK, MLA_HEADS, 32), w_uq.dtype)
    x1 = wq[:, :, MLA_NOPE_DIM:MLA_NOPE_DIM + ROPE_HALF]
    x2 = wq[:, :, MLA_NOPE_DIM + ROPE_HALF:]
    wa = jnp.concatenate([wq, zq], axis=2).reshape(Q_LORA_RANK, QPAD)
    wb = jnp.concatenate([jnp.zeros_like(wq[:, :, :MLA_NOPE_DIM]), -x2, x1, zq], axis=2).reshape(Q_LORA_RANK, QPAD)
    wuq = jnp.concatenate([wa, wb], axis=1).astype(BF16)

    wkv = w_ukv.reshape(KV_LORA_RANK, MLA_HEADS, MLA_NOPE_DIM + MLA_V_DIM)
    wk = jnp.concatenate([wkv[:, :, :MLA_NOPE_DIM], jnp.zeros_like(wkv[:, :, :MLA_NOPE_DIM])], axis=2)
    wv = wkv[:, :, MLA_NOPE_DIM:]
    wukv = jnp.concatenate([wk.reshape(KV_LORA_RANK, QPAD), wv.reshape(KV_LORA_RANK, MLA_WIDTH)],
                           axis=1).astype(BF16)
    return win, wuq, wukv


def _mixer_in(x2d, gpre, win, gq, wuq, gkv, wukv, ct, st, tm=256):
    n = x2d.shape[0]
    row = lambda w: pl.BlockSpec((tm, w), lambda i: (i, 0))
    full = lambda a: pl.BlockSpec(a.shape, lambda i: (0, 0))
    outs = [(QPAD, BF16), (QPAD, BF16), (MLA_WIDTH, BF16), (DIL_WIDTH, BF16), (DIL_WIDTH, BF16), (DIL_WIDTH, BF16)]
    return pl.pallas_call(
        _mixer_in_kernel,
        grid=(n // tm,),
        in_specs=[row(D_MODEL), full(gpre), full(win), full(gq), full(wuq), full(gkv), full(wukv),
                  row(LANES), row(LANES)],
        out_specs=[row(w) for w, _ in outs],
        out_shape=[jax.ShapeDtypeStruct((n, w), dt) for w, dt in outs],
        compiler_params=_params("parallel"),
        name="mixer_in",
    )(x2d, gpre, win, gq, wuq, gkv, wukv, ct, st)


def _mla_flash_kernel(q_ref, k_ref, v_ref, o_ref, *, tq, tk):
    qi = pl.program_id(2)
    q0 = qi * tq
    lane = lax.broadcasted_iota(jnp.int32, (1, LANES), 1)
    lo = lane < MLA_V_DIM
    qs = [q_ref[0, :, 0:LANES], q_ref[0, :, LANES:2 * LANES]]

    def step(j, carry, masked):
        ks = pl.multiple_of(j * tk, tk)
        vblk = v_ref[0, pl.ds(ks, tk), :]
        stats = list(carry[:4])
        acc = carry[4]
        alphas = []
        pvs = []
        for h in range(2):
            m_prev, l_prev = stats[2 * h], stats[2 * h + 1]
            kblk = k_ref[0, pl.ds(ks, tk), h * LANES:(h + 1) * LANES]
            s = _dot_nt(qs[h], kblk)
            if masked:
                row = q0 + lax.broadcasted_iota(jnp.int32, (tq, tk), 0)
                col = ks + lax.broadcasted_iota(jnp.int32, (tq, tk), 1)
                s = jnp.where(col <= row, s, NEG_BIG)
            m_new = jnp.maximum(m_prev, jnp.max(s, axis=-1, keepdims=True))
            alpha = jnp.exp(m_prev - m_new)
            p = jnp.exp(s - m_new)
            stats[2 * h] = m_new
            stats[2 * h + 1] = alpha * l_prev + jnp.sum(p, axis=-1, keepdims=True)
            vh = jnp.where(lo if h == 0 else jnp.logical_not(lo), vblk, jnp.zeros_like(vblk))
            pvs.append(_dot(p.astype(BF16), vh))
            alphas.append(alpha)
        acc = acc * jnp.where(lo, alphas[0], alphas[1]) + pvs[0] + pvs[1]
        return (*stats, acc)

    neg = jnp.full((tq, 1), NEG_BIG, F32)
    zero = jnp.zeros((tq, 1), F32)
    init = (neg, zero, neg, zero, jnp.zeros((tq, LANES), F32))
    n_full = q0 // tk
    carry = lax.fori_loop(0, n_full, lambda j, c: step(j, c, False), init)
    carry = step(n_full, carry, True)
    inv = jnp.where(lo, 1.0 / carry[1], 1.0 / carry[3])
    o_ref[0] = (carry[4] * inv).astype(o_ref.dtype)


def _mla_attention(q, k, v, tq=256, tk=512):
    b, s, _ = q.shape
    assert tk % tq == 0 and s % tk == 0
    return pl.pallas_call(
        functools.partial(_mla_flash_kernel, tq=tq, tk=tk),
        grid=(b, MLA_HEADS // 2, s // tq),
        in_specs=[pl.BlockSpec((1, tq, 2 * LANES), lambda bi, hp, qi: (bi, qi, hp)),
                  pl.BlockSpec((1, s, 2 * LANES), lambda bi, hp, qi: (bi, 0, hp)),
                  pl.BlockSpec((1, s, LANES), lambda bi, hp, qi: (bi, 0, hp))],
        out_specs=pl.BlockSpec((1, tq, LANES), lambda bi, hp, qi: (bi, qi, hp)),
        out_shape=jax.ShapeDtypeStruct((b, s, MLA_WIDTH), BF16),
        compiler_params=_params("parallel", "parallel", "arbitrary"),
        name="mla_flash",
    )(q, k, v)


def _dilated_kernel(q_ref, kp_ref, kc_ref, vp_ref, vc_ref, bias_ref, o_ref, lse_ref):
    c = pl.program_id(0)
    variant = jnp.minimum(c, 1)
    lane = lax.broadcasted_iota(jnp.int32, (1, LANES), 1)
    lo = lane < DIL_HEAD_DIM
    lse_tile = jnp.zeros((DIL_N, LANES), F32)
    for hp in range(DIL_HEADS // 2):
        sl = slice(hp * LANES, (hp + 1) * LANES)
        qp = q_ref[0, :, sl]
        kk = jnp.concatenate([kp_ref[0, :, sl], kc_ref[0, :, sl]], axis=0)
        vv = jnp.concatenate([vp_ref[0, :, sl], vc_ref[0, :, sl]], axis=0)
        out = jnp.zeros((DIL_N, LANES), F32)
        for sub in range(2):
            h = 2 * hp + sub
            sel = lo if sub == 0 else jnp.logical_not(lo)
            s = _dot_nt(jnp.where(sel, qp, jnp.zeros_like(qp)), kk) + bias_ref[variant, h]
            m = jnp.max(s, axis=-1, keepdims=True)
            e = jnp.exp(s - m)
            den = jnp.sum(e, axis=-1, keepdims=True)
            o = _dot(e.astype(BF16), jnp.where(sel, vv, jnp.zeros_like(vv)))
            out = out + o * (1.0 / den)
            lse_tile = jnp.where(lane == h, m + jnp.log(den), lse_tile)
        o_ref[0, :, sl] = out.astype(o_ref.dtype)
    lse_ref[0] = lse_tile


def _dilated_pattern(qd, kd, vd, bias, dil):
    b, s, w = qd.shape
    l = s // dil
    nc = l // DIL_N
    view = lambda a: a.reshape(b, l, dil * w)
    cur = pl.BlockSpec((1, DIL_N, w), lambda c, bi, r: (bi, c, r))
    prev = pl.BlockSpec((1, DIL_N, w), lambda c, bi, r: (bi, jnp.maximum(c - 1, 0), r))
    o, lse = pl.pallas_call(
        _dilated_kernel,
        grid=(nc, b, dil),
        in_specs=[cur, prev, cur, prev, cur,
                  pl.BlockSpec(bias.shape, lambda c, bi, r: (0, 0, 0, 0))],
        out_specs=[pl.BlockSpec((1, DIL_N, w), lambda c, bi, r: (bi, c, r)),
                   pl.BlockSpec((1, DIL_N, LANES), lambda c, bi, r: (bi, c, r))],
        out_shape=[jax.ShapeDtypeStruct((b, l, dil * w), BF16),
                   jax.ShapeDtypeStruct((b, l, dil * LANES), F32)],
        compiler_params=_params("parallel", "parallel", "parallel"),
        name=f"dilated_d{dil}",
    )(view(qd), view(kd), view(kd), view(vd), view(vd), bias)
    return o.reshape(b, s, w), lse.reshape(b, s, LANES)


def _mem_kv_kernel(mem_ref, g_ref, wk_ref, wv_ref, k_ref, v_ref):
    mn = _rms(mem_ref[0], g_ref[...]).astype(BF16)
    k_ref[0] = _dot(mn, wk_ref[...]).astype(BF16)
    v_ref[0] = _dot(mn, wv_ref[...]).astype(BF16)


def _mem_kv(mem, g, wk, wv):
    b, m, d = mem.shape
    full = lambda a: pl.BlockSpec(a.shape, lambda i: (0, 0))
    blk = pl.BlockSpec((1, m, d), lambda i: (i, 0, 0))
    return pl.pallas_call(
        _mem_kv_kernel,
        grid=(b,),
        in_specs=[blk, full(g), full(wk), full(wv)],
        out_specs=[blk, blk],
        out_shape=[jax.ShapeDtypeStruct((b, m, d), BF16)] * 2,
        compiler_params=_params("parallel"),
        name="mem_kv",
    )(mem, g, wk, wv)


def _post_mixer_kernel(x_ref, om_ref, o1_ref, o2_ref, o3_ref, l1_ref, l2_ref, l3_ref, ex_ref,
                       wout_ref, gpm_ref, gpx_ref, wxq_ref, kx_ref, vx_ref, wxo_ref, gpox_ref, out_ref):
    l1, l2, l3 = l1_ref[0], l2_ref[0], l3_ref[0]
    mx = jnp.maximum(jnp.maximum(l1, l2), l3)
    e1, e2, e3 = jnp.exp(l1 - mx), jnp.exp(l2 - mx), jnp.exp(l3 - mx)
    inv = 1.0 / (e1 + e2 + e3)
    ex = ex_ref[...]

    def widen(w):
        hi = w.astype(BF16)
        lo = (w - hi.astype(F32)).astype(BF16)
        return _dot(hi, ex) + _dot(lo, ex)

    o_dil = (widen(e1 * inv) * o1_ref[0].astype(F32) + widen(e2 * inv) * o2_ref[0].astype(F32)
             + widen(e3 * inv) * o3_ref[0].astype(F32))
    a = _dot(om_ref[0], wout_ref[:MLA_WIDTH, :]) + _dot(o_dil.astype(BF16), wout_ref[MLA_WIDTH:, :])
    x1 = x_ref[0] + _rms(a, gpm_ref[...])

    hq = _rms(x1, gpx_ref[...]).astype(BF16)
    q = (_dot(hq, wxq_ref[...]) * (X_HEAD_DIM ** -0.5)).astype(BF16)
    heads = []
    for h in range(X_HEADS):
        sl = slice(h * X_HEAD_DIM, (h + 1) * X_HEAD_DIM)
        s = _dot_nt(q[:, sl], kx_ref[0, :, sl])
        m = jnp.max(s, axis=-1, keepdims=True)
        e = jnp.exp(s - m)
        den = jnp.sum(e, axis=-1, keepdims=True)
        heads.append((_dot(e.astype(BF16), vx_ref[0, :, sl]) * (1.0 / den)).astype(BF16))
    o = jnp.concatenate(heads, axis=-1)
    cx = _dot(o, wxo_ref[...])
    out_ref[0] = x1 + _rms(cx, gpox_ref[...])


def _post_mixer(x, om, os_, lses, ex, wout, gpm, gpx, wxq, kx, vx, wxo, gpox, tm=256):
    b, s, d = x.shape
    tok = lambda w: pl.BlockSpec((1, tm, w), lambda bi, i: (bi, i, 0))
    full = lambda a: pl.BlockSpec(a.shape, lambda bi, i: (0,) * a.ndim)
    memb = pl.BlockSpec((1, MEM_LEN, d), lambda bi, i: (bi, 0, 0))
    return pl.pallas_call(
        _post_mixer_kernel,
        grid=(b, s // tm),
        in_specs=[tok(d), tok(MLA_WIDTH), tok(DIL_WIDTH), tok(DIL_WIDTH), tok(DIL_WIDTH),
                  tok(LANES), tok(LANES), tok(LANES), full(ex),
                  full(wout), full(gpm), full(gpx), full(wxq), memb, memb, full(wxo), full(gpox)],
        out_specs=tok(d),
        out_shape=jax.ShapeDtypeStruct((b, s, d), F32),
        compiler_params=_params("parallel", "parallel"),
        name="post_mixer",
    )(x, om, *os_, *lses, ex, wout, gpm, gpx, wxq, kx, vx, wxo, gpox)


def _mlp_kernel(x_ref, gpre_ref, wup_ref, wdn_ref, gpost_ref, out_ref):
    x = x_ref[...]
    h = _rms(x, gpre_ref[...]).astype(BF16)
    u = jnp.maximum(_dot(h, wup_ref[...]), 0.0)
    y = _dot((u * u).astype(BF16), wdn_ref[...])
    out_ref[...] = x + _rms(y, gpost_ref[...])


def _mlp(x2d, gpre, wup, wdn, gpost, tm=256):
    n, d = x2d.shape
    row = pl.BlockSpec((tm, d), lambda i: (i, 0))
    full = lambda a: pl.BlockSpec(a.shape, lambda i: (0, 0))
    return pl.pallas_call(
        _mlp_kernel,
        grid=(n // tm,),
        in_specs=[row, full(gpre), full(wup), full(wdn), full(gpost)],
        out_specs=row,
        out_shape=jax.ShapeDtypeStruct((n, d), F32),
        compiler_params=_params("parallel"),
        name="mlp",
    )(x2d, gpre, wup, wdn, gpost)


def _head_expander():
    e = np.zeros((LANES, DIL_WIDTH), np.float32)
    for h in range(DIL_HEADS):
        e[h, h * DIL_HEAD_DIM:(h + 1) * DIL_HEAD_DIM] = 1.0
    return jnp.asarray(e, dtype=BF16)


def kernel(x, mem, positions, rel_bias, w_in, q_norm_g, w_uq, kv_norm_g, w_ukv, w_mix_out, pre_mix_g, post_mix_g, w_xq, w_xk, w_xv, w_xo, mem_g, pre_xattn_g, post_xattn_g, w_up, w_down, pre_mlp_g, post_mlp_g):
    b, s, d = x.shape
    n = b * s
    ct, st = _rope_tables(positions)
    bias = _bias_tables(rel_bias)
    ex = _head_expander()
    g = lambda a, l: a[l].reshape(1, -1).astype(F32)
    for l in range(DEPTH):
        win, wuq, wukv = _prep_mixer_weights(w_in[l], w_uq[l], w_ukv[l])
        q, k, v, qd, kd, vd = _mixer_in(x.reshape(n, d), g(pre_mix_g, l), win, g(q_norm_g, l), wuq,
                                        g(kv_norm_g, l), wukv, ct, st)
        om = _mla_attention(q.reshape(b, s, QPAD), k.reshape(b, s, QPAD), v.reshape(b, s, MLA_WIDTH))
        qd, kd, vd = (a.reshape(b, s, DIL_WIDTH) for a in (qd, kd, vd))
        os_, lses = [], []
        for p, (_, dil) in enumerate(DIL_PATTERNS):
            o_p, lse_p = _dilated_pattern(qd, kd, vd, bias[p], dil)
            os_.append(o_p)
            lses.append(lse_p)
        kx, vx = _mem_kv(mem, g(mem_g, l), w_xk[l].astype(BF16), w_xv[l].astype(BF16))
        x = _post_mixer(x, om, os_, lses, ex, w_mix_out[l].astype(BF16), g(post_mix_g, l),
                        g(pre_xattn_g, l), w_xq[l].astype(BF16), kx, vx, w_xo[l].astype(BF16),
                        g(post_xattn_g, l))
        x = _mlp(x.reshape(n, d), g(pre_mlp_g, l), w_up[l].astype(BF16), w_down[l].astype(BF16),
                 g(post_mlp_g, l)).reshape(b, s, d)
    return x
```

```python
import functools
import math

import numpy as np
import jax
import jax.numpy as jnp
from jax import lax
from jax.experimental import pallas as pl
from jax.experimental.pallas import tpu as pltpu

D_MODEL = 1024
DEPTH = 2
MEM_LEN = 256
MLA_HEADS = 8
MLA_NOPE_DIM = 64
MLA_ROPE_DIM = 32
MLA_V_DIM = 64
Q_LORA_RANK = 384
KV_LORA_RANK = 256
ROPE_THETA = 10000.0
DIL_HEADS = 8
DIL_HEAD_DIM = 64
DIL_PATTERNS = ((128, 1), (512, 4), (2048, 16))
N_BUCKETS = 32
MAX_DISTANCE = 2048
X_HEADS = 4
X_HEAD_DIM = D_MODEL // X_HEADS
D_FF = 4 * D_MODEL
EPS = 1e-6

MLA_QK_DIM = MLA_NOPE_DIM + MLA_ROPE_DIM
DIL_WIDTH = DIL_HEADS * DIL_HEAD_DIM
MLA_WIDTH = MLA_HEADS * MLA_V_DIM
ROPE_HALF = MLA_ROPE_DIM // 2
DIL_N = 128
LANES = 128
HALF = LANES // 2
NEG_BIG = -1e30
LOG2E = math.log2(math.e)
VMEM_LIMIT_BYTES = 56 * 1024 * 1024

BF16 = jnp.bfloat16
F32 = jnp.float32


def _rms(x, g):
    return x * lax.rsqrt(jnp.mean(x * x, axis=-1, keepdims=True) + EPS) * g


def _dot(a, b):
    return jnp.dot(a, b, preferred_element_type=F32)


def _dot_nt(a, b):
    return lax.dot_general(a, b, (((1,), (1,)), ((), ())), preferred_element_type=F32)


def _params(*sem):
    return pltpu.CompilerParams(dimension_semantics=sem, vmem_limit_bytes=VMEM_LIMIT_BYTES)


def _low_half():
    return lax.broadcasted_iota(jnp.int32, (1, LANES), 1) < HALF


def _rope_table_kernel(pos_ref, inv_ref, cos_ref, sin_ref):
    ang = pos_ref[...] * inv_ref[...]
    cos_ref[...] = jnp.cos(ang)
    sin_ref[...] = jnp.sin(ang)


def _rope_tables(positions):
    n = positions.size
    rows = n // 8
    pos_rep = jnp.repeat(positions.reshape(rows, 8).astype(F32), ROPE_HALF, axis=1)
    inv = ROPE_THETA ** (-jnp.arange(ROPE_HALF, dtype=F32) / ROPE_HALF)
    inv_rep = jnp.tile(inv, 8).reshape(1, LANES)
    tr = 256
    cos, sin = pl.pallas_call(
        _rope_table_kernel,
        grid=(rows // tr,),
        in_specs=[pl.BlockSpec((tr, LANES), lambda i: (i, 0)),
                  pl.BlockSpec((1, LANES), lambda i: (0, 0))],
        out_specs=[pl.BlockSpec((tr, LANES), lambda i: (i, 0))] * 2,
        out_shape=[jax.ShapeDtypeStruct((rows, LANES), F32)] * 2,
        compiler_params=_params("parallel"),
        name="rope_tables",
    )(pos_rep, inv_rep)
    cos = cos.reshape(n, ROPE_HALF)
    sin = sin.reshape(n, ROPE_HALF)
    ct = jnp.concatenate([jnp.ones((n, MLA_NOPE_DIM), F32), cos, cos, jnp.zeros((n, 32), F32)], axis=1)
    st = jnp.concatenate([jnp.zeros((n, MLA_NOPE_DIM), F32), sin, sin, jnp.zeros((n, 32), F32)], axis=1)
    return ct, st


def _bucket_tables():
    i = np.arange(DIL_N)[:, None]
    m = np.arange(2 * DIL_N)[None, :]
    rel = DIL_N + i - m
    band = (rel >= 0) & (rel <= DIL_N)
    max_exact = N_BUCKETS // 2
    out = []
    for _, dil in DIL_PATTERNS:
        d = np.maximum(rel * dil, 0)
        df = np.maximum(d.astype(np.float64), 1.0)
        large = max_exact + (np.log(df / max_exact) / math.log(MAX_DISTANCE / max_exact)
                             * (N_BUCKETS - max_exact)).astype(np.int32)
        large = np.minimum(large, N_BUCKETS - 1)
        bucket = np.where(d < max_exact, d, large)
        out.append(np.where(band, bucket, -1).astype(np.int32))
    return np.stack(out)


def _bias_table_kernel(rb_ref, bucket_ref, out_ref):
    h = pl.program_id(1)
    bucket = bucket_ref[0]
    tab = jnp.full(bucket.shape, NEG_BIG, F32)
    for b in range(N_BUCKETS):
        tab = jnp.where(bucket == b, rb_ref[b, h] * LOG2E, tab)
    col = lax.broadcasted_iota(jnp.int32, bucket.shape, 1)
    out_ref[0, 0, 0] = jnp.where(col >= DIL_N, tab, NEG_BIG)
    out_ref[0, 1, 0] = tab


def _bias_tables(rel_bias):
    buckets = jnp.asarray(_bucket_tables())
    npat = len(DIL_PATTERNS)
    return pl.pallas_call(
        _bias_table_kernel,
        grid=(npat, DIL_HEADS),
        in_specs=[pl.BlockSpec(memory_space=pltpu.SMEM),
                  pl.BlockSpec((1, DIL_N, 2 * DIL_N), lambda p, h: (p, 0, 0))],
        out_specs=pl.BlockSpec((1, 2, 1, DIL_N, 2 * DIL_N), lambda p, h: (p, 0, h, 0, 0)),
        out_shape=jax.ShapeDtypeStruct((npat, 2, DIL_HEADS, DIL_N, 2 * DIL_N), F32),
        compiler_params=_params("parallel", "parallel"),
        name="bias_tables",
    )(rel_bias.astype(F32), buckets)


C_KRA = Q_LORA_RANK + KV_LORA_RANK
C_KRB = C_KRA + LANES
C_QD = C_KRB + LANES
C_KD = C_QD + DIL_WIDTH
C_VD = C_KD + DIL_WIDTH
IN_EXT = C_VD + DIL_WIDTH
QPAD = MLA_HEADS * LANES


def _mixer_in_kernel(x_ref, gpre_ref, win_ref, gq_ref, wuq_ref, gkv_ref, wukv_ref, ct_ref, st_ref,
                     q_ref, k_ref, v_ref, qd_ref, kd_ref, vd_ref):
    xn = _rms(x_ref[...], gpre_ref[...])
    z = _dot(xn.astype(BF16), win_ref[...])
    ct = ct_ref[...]
    st = st_ref[...]

    qd_ref[...] = (z[:, C_QD:C_KD] * (DIL_HEAD_DIM ** -0.5 * LOG2E)).astype(BF16)
    kd_ref[...] = z[:, C_KD:C_VD].astype(BF16)
    vd_ref[...] = z[:, C_VD:IN_EXT].astype(BF16)

    cqn = _rms(z[:, :Q_LORA_RANK], gq_ref[...])
    q2 = _dot(cqn.astype(BF16), wuq_ref[...])
    scale = MLA_QK_DIM ** -0.5 * LOG2E
    for h in range(MLA_HEADS):
        a = q2[:, h * LANES:(h + 1) * LANES]
        b = q2[:, QPAD + h * LANES:QPAD + (h + 1) * LANES]
        q_ref[:, h * LANES:(h + 1) * LANES] = ((a * ct + b * st) * scale).astype(BF16)

    ckvn = _rms(z[:, Q_LORA_RANK:C_KRA], gkv_ref[...])
    kv = _dot(ckvn.astype(BF16), wukv_ref[...])
    v_ref[...] = kv[:, QPAD:].astype(BF16)
    kpe = z[:, C_KRA:C_KRB] * ct + z[:, C_KRB:C_QD] * st
    for h in range(MLA_HEADS):
        k_ref[:, h * LANES:(h + 1) * LANES] = (kv[:, h * LANES:(h + 1) * LANES] + kpe).astype(BF16)


def _prep_mixer_weights(w_in, w_uq, w_ukv):
    o1, o2 = Q_LORA_RANK, Q_LORA_RANK + KV_LORA_RANK
    o3 = o2 + MLA_ROPE_DIM
    k1 = w_in[:, o2:o2 + ROPE_HALF]
    k2 = w_in[:, o2 + ROPE_HALF:o3]
    zc = lambda n: jnp.zeros((w_in.shape[0], n), w_in.dtype)
    kra = jnp.concatenate([zc(MLA_NOPE_DIM), k1, k2, zc(32)], axis=1)
    krb = jnp.concatenate([zc(MLA_NOPE_DIM), -k2, k1, zc(32)], axis=1)
    win = jnp.concatenate([w_in[:, :o2], kra, krb, w_in[:, o3:]], axis=1).astype(BF16)

    wq = w_uq.reshape(Q_LORA_RANK, MLA_HEADS, MLA_QK_DIM)
    zq = jnp.zeros((Q_LORA_RANK, MLA_HEADS, 32), w_uq.dtype)
    x1 = wq[:, :, MLA_NOPE_DIM:MLA_NOPE_DIM + ROPE_HALF]
    x2 = wq[:, :, MLA_NOPE_DIM + ROPE_HALF:]
    wa = jnp.concatenate([wq, zq], axis=2).reshape(Q_LORA_RANK, QPAD)
    wb = jnp.concatenate([jnp.zeros_like(wq[:, :, :MLA_NOPE_DIM]), -x2, x1, zq], axis=2).reshape(Q_LORA_RANK, QPAD)
    wuq = jnp.concatenate([wa, wb], axis=1).astype(BF16)

    wkv = w_ukv.reshape(KV_LORA_RANK, MLA_HEADS, MLA_NOPE_DIM + MLA_V_DIM)
    wk = jnp.concatenate([wkv[:, :, :MLA_NOPE_DIM], jnp.zeros_like(wkv[:, :, :MLA_NOPE_DIM])], axis=2)
    wv = wkv[:, :, MLA_NOPE_DIM:]
    wukv = jnp.concatenate([wk.reshape(KV_LORA_RANK, QPAD), wv.reshape(KV_LORA_RANK, MLA_WIDTH)],
                           axis=1).astype(BF16)
    return win, wuq, wukv


def _mixer_in(x2d, gpre, win, gq, wuq, gkv, wukv, ct, st, tm=256):
    n = x2d.shape[0]
    row = lambda w: pl.BlockSpec((tm, w), lambda i: (i, 0))
    full = lambda a: pl.BlockSpec(a.shape, lambda i: (0, 0))
    outs = [(QPAD, BF16), (QPAD, BF16), (MLA_WIDTH, BF16), (DIL_WIDTH, BF16), (DIL_WIDTH, BF16), (DIL_WIDTH, BF16)]
    return pl.pallas_call(
        _mixer_in_kernel,
        grid=(n // tm,),
        in_specs=[row(D_MODEL), full(gpre), full(win), full(gq), full(wuq), full(gkv), full(wukv),
                  row(LANES), row(LANES)],
        out_specs=[row(w) for w, _ in outs],
        out_shape=[jax.ShapeDtypeStruct((n, w), dt) for w, dt in outs],
        compiler_params=_params("parallel"),
        name="mixer_in",
    )(x2d, gpre, win, gq, wuq, gkv, wukv, ct, st)


def _mla_attn_kernel(q_ref, k_ref, v_ref, o_ref, s_scr, m_scr, acc_scr, *, tq):
    qi = pl.program_id(2)
    lo = _low_half()
    nlt = tq // LANES

    def lane_tile_max(s):
        m = s[:, :LANES]
        for t in range(1, nlt):
            m = jnp.maximum(m, s[:, t * LANES:(t + 1) * LANES])
        return m

    row_max = []
    for h in range(2):
        hs = slice(h * LANES, (h + 1) * LANES)
        qh = q_ref[0, :, hs]
        m_scr[...] = jnp.full((tq, LANES), NEG_BIG, F32)

        @pl.loop(0, qi)
        def _(j):
            ks = pl.multiple_of(j * tq, tq)
            s = _dot_nt(qh, k_ref[0, pl.ds(ks, tq), hs])
            s_scr[h, j] = s
            m_scr[...] = jnp.maximum(m_scr[...], lane_tile_max(s))

        ks = pl.multiple_of(qi * tq, tq)
        s = _dot_nt(qh, k_ref[0, pl.ds(ks, tq), hs])
        row = lax.broadcasted_iota(jnp.int32, (tq, tq), 0)
        col = lax.broadcasted_iota(jnp.int32, (tq, tq), 1)
        s = jnp.where(col <= row, s, NEG_BIG)
        s_scr[h, qi] = s
        row_max.append(jnp.max(jnp.maximum(m_scr[...], lane_tile_max(s)), axis=-1, keepdims=True))

    acc_scr[...] = jnp.zeros_like(acc_scr)

    @pl.loop(0, qi + 1)
    def _(j):
        ks = pl.multiple_of(j * tq, tq)
        vblk = v_ref[0, pl.ds(ks, tq), :]
        for h in range(2):
            sel = lo if h == 0 else jnp.logical_not(lo)
            p = jnp.exp2(s_scr[h, j] - row_max[h]).astype(BF16)
            acc_scr[h] += _dot(p, jnp.where(sel, vblk, jnp.ones_like(vblk)))

    r0 = acc_scr[0]
    r1 = acc_scr[1]
    o = jnp.where(lo, r0 / pltpu.roll(r0, HALF, axis=1), r1 / pltpu.roll(r1, HALF, axis=1))
    o_ref[0] = o.astype(o_ref.dtype)


def _mla_attention(q, k, v, tq=512):
    b, s, _ = q.shape
    nq = s // tq
    return pl.pallas_call(
        functools.partial(_mla_attn_kernel, tq=tq),
        grid=(b, MLA_HEADS // 2, nq),
        in_specs=[pl.BlockSpec((1, tq, 2 * LANES), lambda bi, hp, qi: (bi, qi, hp)),
                  pl.BlockSpec((1, s, 2 * LANES), lambda bi, hp, qi: (bi, 0, hp)),
                  pl.BlockSpec((1, s, LANES), lambda bi, hp, qi: (bi, 0, hp))],
        out_specs=pl.BlockSpec((1, tq, LANES), lambda bi, hp, qi: (bi, qi, hp)),
        out_shape=jax.ShapeDtypeStruct((b, s, MLA_WIDTH), BF16),
        scratch_shapes=[pltpu.VMEM((2, nq, tq, tq), F32),
                        pltpu.VMEM((tq, LANES), F32),
                        pltpu.VMEM((2, tq, LANES), F32)],
        compiler_params=_params("parallel", "parallel", "arbitrary"),
        name="mla_attn",
    )(q, k, v)


def _dil_pair(q, kk, vv, bias_pair, lo):
    halves = []
    for sub in range(2):
        sel = lo if sub == 0 else jnp.logical_not(lo)
        s = _dot_nt(jnp.where(sel, q, jnp.zeros_like(q)), kk) + bias_pair[sub]
        m = jnp.max(s, axis=-1, keepdims=True)
        e = jnp.exp2(s - m).astype(BF16)
        r = _dot(e, jnp.where(sel, vv, jnp.ones_like(vv)))
        den = pltpu.roll(r, HALF, axis=1)
        halves.append((r / den, m + jnp.log2(den)))
    o = jnp.where(lo, halves[0][0], halves[1][0])
    lse = jnp.where(lo, halves[0][1], halves[1][1])
    return o, lse


def _dilated_kernel(q_ref, k_ref, v_ref, bias_ref, o_ref, qf, kf, vf,
                    o1, o2, o3, l1, l2, l3, *, seq):
    lo = _low_half()
    qf[...] = q_ref[0].astype(F32)
    kf[...] = k_ref[0].astype(F32)
    vf[...] = v_ref[0].astype(F32)
    o_scr = (o1, o2, o3)
    l_scr = (l1, l2, l3)

    def rows(start, n, dil):
        return pl.ds(start, n) if dil == 1 else pl.ds(start, n, stride=dil)

    def load(ref_bf, ref_f, start, n, dil):
        if dil == 1:
            return ref_bf[0, pl.ds(start, n), :]
        return ref_f[rows(start, n, dil), :].astype(BF16)

    def emit(p, start, dil, o, lse):
        o_scr[p][rows(start, DIL_N, dil), :] = o
        l_scr[p][rows(start, DIL_N, dil), :] = lse

    def first_chunk(p, dil, r):
        b = (bias_ref[p, 0, 0, :, DIL_N:], bias_ref[p, 0, 1, :, DIL_N:])
        q = load(q_ref, qf, r, DIL_N, dil)
        kk = load(k_ref, kf, r, DIL_N, dil)
        vv = load(v_ref, vf, r, DIL_N, dil)
        emit(p, r, dil, *_dil_pair(q, kk, vv, b, lo))

    def later_chunk(p, dil, r, c):
        span = DIL_N * dil
        qs = r + span * c
        ks = qs - span
        if dil == 1:
            qs = pl.multiple_of(qs, DIL_N)
            ks = pl.multiple_of(ks, DIL_N)
        b = (bias_ref[p, 1, 0], bias_ref[p, 1, 1])
        q = load(q_ref, qf, qs, DIL_N, dil)
        kk = load(k_ref, kf, ks, 2 * DIL_N, dil)
        vv = load(v_ref, vf, ks, 2 * DIL_N, dil)
        emit(p, qs, dil, *_dil_pair(q, kk, vv, b, lo))

    for p, (window, dil) in enumerate(DIL_PATTERNS):
        nc = seq // window
        if dil == 1:
            first_chunk(p, dil, 0)
        else:
            half = dil // 2

            @pl.loop(0, half)
            def _(r):
                first_chunk(p, dil, r)
                first_chunk(p, dil, r + half)

        later = dil * (nc - 1)
        shift = dil.bit_length() - 1
        if later > 0:
            if later % 2 == 0:
                @pl.loop(0, later // 2)
                def _(i):
                    for idx in (i, i + later // 2):
                        later_chunk(p, dil, idx & (dil - 1), 1 + (idx >> shift))
            else:
                @pl.loop(0, later)
                def _(i):
                    later_chunk(p, dil, i & (dil - 1), 1 + (i >> shift))

    tr = 256

    @pl.loop(0, seq // tr)
    def _(i):
        rs = pl.ds(pl.multiple_of(i * tr, tr), tr)
        ls = [l[rs, :] for l in l_scr]
        mx = jnp.maximum(jnp.maximum(ls[0], ls[1]), ls[2])
        es = [jnp.exp2(l - mx) for l in ls]
        num = es[0] * o1[rs, :] + es[1] * o2[rs, :] + es[2] * o3[rs, :]
        o_ref[0, rs, :] = (num / (es[0] + es[1] + es[2])).astype(o_ref.dtype)


def _dilated_attention(qd, kd, vd, bias):
    b, s, w = qd.shape
    blk = pl.BlockSpec((1, s, LANES), lambda bi, hp: (bi, 0, hp))
    npat = len(DIL_PATTERNS)
    return pl.pallas_call(
        functools.partial(_dilated_kernel, seq=s),
        grid=(b, DIL_HEADS // 2),
        in_specs=[blk, blk, blk,
                  pl.BlockSpec((npat, 2, 2, DIL_N, 2 * DIL_N), lambda bi, hp: (0, 0, hp, 0, 0))],
        out_specs=blk,
        out_shape=jax.ShapeDtypeStruct((b, s, w), BF16),
        scratch_shapes=[pltpu.VMEM((s, LANES), F32)] * (3 + 2 * npat),
        compiler_params=_params("parallel", "parallel"),
        name="dilated",
    )(qd, kd, vd, bias)


def _mem_kv_kernel(mem_ref, g_ref, wk_ref, wv_ref, k_ref, v_ref):
    mn = _rms(mem_ref[0], g_ref[...]).astype(BF16)
    k_ref[0] = _dot(mn, wk_ref[...]).astype(BF16)
    v_ref[0] = _dot(mn, wv_ref[...]).astype(BF16)


def _mem_kv(mem, g, wk, wv):
    b, m, d = mem.shape
    full = lambda a: pl.BlockSpec(a.shape, lambda i: (0, 0))
    blk = pl.BlockSpec((1, m, d), lambda i: (i, 0, 0))
    return pl.pallas_call(
        _mem_kv_kernel,
        grid=(b,),
        in_specs=[blk, full(g), full(wk), full(wv)],
        out_specs=[blk, blk],
        out_shape=[jax.ShapeDtypeStruct((b, m, d), BF16)] * 2,
        compiler_params=_params("parallel"),
        name="mem_kv",
    )(mem, g, wk, wv)


def _post_mixer_kernel(x_ref, om_ref, od_ref, wout_ref, gpm_ref, gpx_ref, wxq_ref, kx_ref, vx_ref,
                       wxo_ref, gpox_ref, out_ref):
    a = _dot(om_ref[0], wout_ref[:MLA_WIDTH, :]) + _dot(od_ref[0], wout_ref[MLA_WIDTH:, :])
    x1 = x_ref[0] + _rms(a, gpm_ref[...])

    hq = _rms(x1, gpx_ref[...]).astype(BF16)
    q = (_dot(hq, wxq_ref[...]) * (X_HEAD_DIM ** -0.5 * LOG2E)).astype(BF16)
    heads = []
    for h in range(X_HEADS):
        sl = slice(h * X_HEAD_DIM, (h + 1) * X_HEAD_DIM)
        s = _dot_nt(q[:, sl], kx_ref[0, :, sl])
        m = jnp.max(s, axis=-1, keepdims=True)
        e = jnp.exp2(s - m)
        den = jnp.sum(e, axis=-1, keepdims=True)
        heads.append((_dot(e.astype(BF16), vx_ref[0, :, sl]) * (1.0 / den)).astype(BF16))
    o = jnp.concatenate(heads, axis=-1)
    cx = _dot(o, wxo_ref[...])
    out_ref[0] = x1 + _rms(cx, gpox_ref[...])


def _post_mixer(x, om, od, wout, gpm, gpx, wxq, kx, vx, wxo, gpox, tm=256):
    b, s, d = x.shape
    tok = lambda w: pl.BlockSpec((1, tm, w), lambda bi, i: (bi, i, 0))
    full = lambda a: pl.BlockSpec(a.shape, lambda bi, i: (0,) * a.ndim)
    memb = pl.BlockSpec((1, MEM_LEN, d), lambda bi, i: (bi, 0, 0))
    return pl.pallas_call(
        _post_mixer_kernel,
        grid=(b, s // tm),
        in_specs=[tok(d), tok(MLA_WIDTH), tok(DIL_WIDTH),
                  full(wout), full(gpm), full(gpx), full(wxq), memb, memb, full(wxo), full(gpox)],
        out_specs=tok(d),
        out_shape=jax.ShapeDtypeStruct((b, s, d), F32),
        compiler_params=_params("parallel", "parallel"),
        name="post_mixer",
    )(x, om, od, wout, gpm, gpx, wxq, kx, vx, wxo, gpox)


def _mlp_kernel(x_ref, gpre_ref, wup_ref, wdn_ref, gpost_ref, out_ref):
    x = x_ref[...]
    h = _rms(x, gpre_ref[...]).astype(BF16)
    u = jnp.maximum(_dot(h, wup_ref[...]), 0.0)
    y = _dot((u * u).astype(BF16), wdn_ref[...])
    out_ref[...] = x + _rms(y, gpost_ref[...])


def _mlp(x2d, gpre, wup, wdn, gpost, tm=256):
    n, d = x2d.shape
    row = pl.BlockSpec((tm, d), lambda i: (i, 0))
    full = lambda a: pl.BlockSpec(a.shape, lambda i: (0, 0))
    return pl.pallas_call(
        _mlp_kernel,
        grid=(n // tm,),
        in_specs=[row, full(gpre), full(wup), full(wdn), full(gpost)],
        out_specs=row,
        out_shape=jax.ShapeDtypeStruct((n, d), F32),
        compiler_params=_params("parallel"),
        name="mlp",
    )(x2d, gpre, wup, wdn, gpost)


def kernel(x, mem, positions, rel_bias, w_in, q_norm_g, w_uq, kv_norm_g, w_ukv, w_mix_out, pre_mix_g, post_mix_g, w_xq, w_xk, w_xv, w_xo, mem_g, pre_xattn_g, post_xattn_g, w_up, w_down, pre_mlp_g, post_mlp_g):
    b, s, d = x.shape
    n = b * s
    ct, st = _rope_tables(positions)
    bias = _bias_tables(rel_bias)
    g = lambda a, l: a[l].reshape(1, -1).astype(F32)
    for l in range(DEPTH):
        win, wuq, wukv = _prep_mixer_weights(w_in[l], w_uq[l], w_ukv[l])
        q, k, v, qd, kd, vd = _mixer_in(x.reshape(n, d), g(pre_mix_g, l), win, g(q_norm_g, l), wuq,
                                        g(kv_norm_g, l), wukv, ct, st)
        om = _mla_attention(q.reshape(b, s, QPAD), k.reshape(b, s, QPAD), v.reshape(b, s, MLA_WIDTH))
        od = _dilated_attention(*(a.reshape(b, s, DIL_WIDTH) for a in (qd, kd, vd)), bias)
        kx, vx = _mem_kv(mem, g(mem_g, l), w_xk[l].astype(BF16), w_xv[l].astype(BF16))
        x = _post_mixer(x, om, od, w_mix_out[l].astype(BF16), g(post_mix_g, l),
                        g(pre_xattn_g, l), w_xq[l].astype(BF16), kx, vx, w_xo[l].astype(BF16),
                        g(post_xattn_g, l))
        x = _mlp(x.reshape(n, d), g(pre_mlp_g, l), w_up[l].astype(BF16), w_down[l].astype(BF16),
                 g(post_mlp_g, l)).reshape(b, s, d)
    return x
```

```python
import functools
import math

import numpy as np
import jax
import jax.numpy as jnp
from jax import lax
from jax.experimental import pallas as pl
from jax.experimental.pallas import tpu as pltpu

D_MODEL = 1024
DEPTH = 2
MEM_LEN = 256
MLA_HEADS = 8
MLA_NOPE_DIM = 64
MLA_ROPE_DIM = 32
MLA_V_DIM = 64
Q_LORA_RANK = 384
KV_LORA_RANK = 256
ROPE_THETA = 10000.0
DIL_HEADS = 8
DIL_HEAD_DIM = 64
DIL_PATTERNS = ((128, 1), (512, 4), (2048, 16))
N_BUCKETS = 32
MAX_DISTANCE = 2048
X_HEADS = 4
X_HEAD_DIM = D_MODEL // X_HEADS
D_FF = 4 * D_MODEL
EPS = 1e-6

MLA_QK_DIM = MLA_NOPE_DIM + MLA_ROPE_DIM
DIL_WIDTH = DIL_HEADS * DIL_HEAD_DIM
MLA_WIDTH = MLA_HEADS * MLA_V_DIM
ROPE_HALF = MLA_ROPE_DIM // 2
DIL_N = 128
LANES = 128
HALF = LANES // 2
ROW_SUBTILE = 256
NEG_BIG = -1e30
LOG2E = math.log2(math.e)
VMEM_LIMIT_BYTES = 56 * 1024 * 1024

BF16 = jnp.bfloat16
F32 = jnp.float32


def _rms(x, g):
    return x * lax.rsqrt(jnp.mean(x * x, axis=-1, keepdims=True) + EPS) * g


def _dot(a, b):
    return jnp.dot(a, b, preferred_element_type=F32)


def _dot_nt(a, b):
    return lax.dot_general(a, b, (((1,), (1,)), ((), ())), preferred_element_type=F32)


def _params(*sem):
    return pltpu.CompilerParams(dimension_semantics=sem, vmem_limit_bytes=VMEM_LIMIT_BYTES)


def _layer_spec(a, l):
    return pl.BlockSpec((None,) + a.shape[1:], lambda *_: (l,) + (0,) * (a.ndim - 1),
                        pipeline_mode=pl.Buffered(1))


def _low_half():
    return lax.broadcasted_iota(jnp.int32, (1, LANES), 1) < HALF


TOKENS_PER_ROW = LANES // ROPE_HALF


def _rope_table_kernel(pos_ref, inv_ref, ct_ref, st_ref):
    tr = pos_ref.shape[0]
    ang = pos_ref[...] * inv_ref[...]
    cos = jnp.cos(ang)
    sin = jnp.sin(ang)
    lane = lax.broadcasted_iota(jnp.int32, (1, LANES), 1)
    x1 = jnp.logical_and(lane >= MLA_NOPE_DIM, lane < MLA_NOPE_DIM + ROPE_HALF)
    x2 = jnp.logical_and(lane >= MLA_NOPE_DIM + ROPE_HALF, lane < MLA_QK_DIM)
    for j in range(TOKENS_PER_ROW):
        to_x1 = (MLA_NOPE_DIM - ROPE_HALF * j) % LANES
        to_x2 = (MLA_NOPE_DIM + ROPE_HALF - ROPE_HALF * j) % LANES
        c1, c2 = pltpu.roll(cos, to_x1, axis=1), pltpu.roll(cos, to_x2, axis=1)
        s1, s2 = pltpu.roll(sin, to_x1, axis=1), pltpu.roll(sin, to_x2, axis=1)
        rows = pl.ds(j, tr, stride=TOKENS_PER_ROW)
        ct_ref[rows, :] = jnp.where(lane < MLA_NOPE_DIM, 1.0, jnp.where(x1, c1, jnp.where(x2, c2, 0.0)))
        st_ref[rows, :] = jnp.where(x1, -s1, jnp.where(x2, s2, 0.0))


def _rope_tables(positions):
    n = positions.size
    rows = n // TOKENS_PER_ROW
    pos_rep = jnp.repeat(positions.reshape(rows, TOKENS_PER_ROW).astype(F32), ROPE_HALF, axis=1)
    inv = ROPE_THETA ** (-jnp.arange(ROPE_HALF, dtype=F32) / ROPE_HALF)
    inv_rep = jnp.tile(inv, TOKENS_PER_ROW).reshape(1, LANES)
    tr = 256
    return pl.pallas_call(
        _rope_table_kernel,
        grid=(rows // tr,),
        in_specs=[pl.BlockSpec((tr, LANES), lambda i: (i, 0)),
                  pl.BlockSpec((1, LANES), lambda i: (0, 0))],
        out_specs=[pl.BlockSpec((tr * TOKENS_PER_ROW, LANES), lambda i: (i, 0))] * 2,
        out_shape=[jax.ShapeDtypeStruct((n, LANES), F32)] * 2,
        compiler_params=_params("parallel"),
        name="rope_tables",
    )(pos_rep, inv_rep)


def _bucket_tables():
    i = np.arange(DIL_N)[:, None]
    m = np.arange(2 * DIL_N)[None, :]
    rel = DIL_N + i - m
    band = (rel >= 0) & (rel <= DIL_N)
    max_exact = N_BUCKETS // 2
    out = []
    for _, dil in DIL_PATTERNS:
        d = np.maximum(rel * dil, 0)
        df = np.maximum(d.astype(np.float64), 1.0)
        large = max_exact + (np.log(df / max_exact) / math.log(MAX_DISTANCE / max_exact)
                             * (N_BUCKETS - max_exact)).astype(np.int32)
        large = np.minimum(large, N_BUCKETS - 1)
        bucket = np.where(d < max_exact, d, large)
        out.append(np.where(band, bucket, -1).astype(np.int32))
    return np.stack(out)


def _bias_table_kernel(rb_ref, bucket_ref, out_ref):
    bucket = bucket_ref[0]
    masked = jnp.full((DIL_N, DIL_N), NEG_BIG, F32)
    for h in range(DIL_HEADS):
        tab = jnp.full(bucket.shape, NEG_BIG, F32)
        for b in range(N_BUCKETS):
            tab = jnp.where(bucket == b, rb_ref[b, h] * LOG2E, tab)
        out_ref[0, 0, h] = jnp.concatenate([tab[:, DIL_N:], masked], axis=1)
        out_ref[0, 1, h] = tab


def _bias_tables(rel_bias):
    buckets = jnp.asarray(_bucket_tables())
    npat = len(DIL_PATTERNS)
    return pl.pallas_call(
        _bias_table_kernel,
        grid=(npat,),
        in_specs=[pl.BlockSpec(memory_space=pltpu.SMEM),
                  pl.BlockSpec((1, DIL_N, 2 * DIL_N), lambda p: (p, 0, 0))],
        out_specs=pl.BlockSpec((1, 2, DIL_HEADS, DIL_N, 2 * DIL_N), lambda p: (p, 0, 0, 0, 0)),
        out_shape=jax.ShapeDtypeStruct((npat, 2, DIL_HEADS, DIL_N, 2 * DIL_N), F32),
        compiler_params=_params("parallel"),
        name="bias_tables",
    )(rel_bias.astype(F32), buckets)


C_KR = Q_LORA_RANK + KV_LORA_RANK
C_QD = C_KR + LANES
C_KD = C_QD + DIL_WIDTH
C_VD = C_KD + DIL_WIDTH
IN_EXT = C_VD + DIL_WIDTH
QPAD = MLA_HEADS * LANES
DIL_SLABS = 3 * DIL_WIDTH // LANES
STRIDED_DILATIONS = tuple(d for _, d in DIL_PATTERNS if d > 1)


def _rotary(a, ct, st, x1_lanes):
    partner = jnp.where(x1_lanes, pltpu.roll(a, LANES - ROPE_HALF, axis=1), pltpu.roll(a, ROPE_HALF, axis=1))
    return a * ct + partner * st


def _mixer_in_kernel(x_ref, gpre_ref, win_ref, gq_ref, wuq_ref, gkv_ref, wukv_ref, ct_ref, st_ref,
                     qkv_ref, nat_ref, d4_ref, d16_ref, slab_scr, cls_scr):
    lane = lax.broadcasted_iota(jnp.int32, (1, LANES), 1)
    x1_lanes = jnp.logical_and(lane >= MLA_NOPE_DIM, lane < MLA_NOPE_DIM + ROPE_HALF)
    nsec = DIL_SLABS // 3

    subs = [slice(c * ROW_SUBTILE, (c + 1) * ROW_SUBTILE) for c in range(x_ref.shape[0] // ROW_SUBTILE)]
    zs = [_dot(_rms(x_ref[rs, :], gpre_ref[...]).astype(BF16), win_ref[...]) for rs in subs]
    for c, (rs, z) in enumerate(zip(subs, zs)):
        ct = ct_ref[rs, :]
        st = st_ref[rs, :]

        per4, per16 = ROW_SUBTILE // 4, ROW_SUBTILE // 16
        for t2 in range(DIL_SLABS // 2):
            vals = []
            for t in (2 * t2, 2 * t2 + 1):
                val = z[:, C_QD + t * LANES:C_QD + (t + 1) * LANES]
                if t < nsec:
                    val = val * (DIL_HEAD_DIM ** -0.5 * LOG2E)
                nat_ref[rs, t * LANES:(t + 1) * LANES] = val.astype(BF16)
                vals.append(val)
            slab_scr[c, t2] = pltpu.pack_elementwise(vals, packed_dtype=BF16)

            def emit(dref, r, rows, packed):
                for i in range(2):
                    tile = pltpu.unpack_elementwise(packed, index=i, packed_dtype=BF16, unpacked_dtype=F32)
                    dref[0, r, rows, (2 * t2 + i) * LANES:(2 * t2 + i + 1) * LANES] = tile.astype(BF16)

            for r4 in range(4):
                cls = slab_scr[c, t2, pl.ds(r4, per4, stride=4), :]
                cls_scr[c, t2, r4] = cls
                emit(d4_ref, r4, slice(c * per4, (c + 1) * per4), cls)
            for r4 in range(4):
                for hi in range(4):
                    emit(d16_ref, r4 + 4 * hi, slice(c * per16, (c + 1) * per16),
                         cls_scr[c, t2, r4, pl.ds(hi, per16, stride=4), :])

        cqn = _rms(z[:, :Q_LORA_RANK], gq_ref[...])
        qa = _dot(cqn.astype(BF16), wuq_ref[...])
        scale = MLA_QK_DIM ** -0.5 * LOG2E
        for h in range(MLA_HEADS):
            hs = slice(h * LANES, (h + 1) * LANES)
            qkv_ref[rs, hs] = (_rotary(qa[:, hs], ct, st, x1_lanes) * scale).astype(BF16)

        ckvn = _rms(z[:, Q_LORA_RANK:C_KR], gkv_ref[...])
        kv = _dot(ckvn.astype(BF16), wukv_ref[...])
        qkv_ref[rs, 2 * QPAD:] = kv[:, QPAD:].astype(BF16)
        kpe = _rotary(z[:, C_KR:C_QD], ct, st, x1_lanes)
        for h in range(MLA_HEADS):
            hs = slice(h * LANES, (h + 1) * LANES)
            qkv_ref[rs, QPAD + h * LANES:QPAD + (h + 1) * LANES] = (kv[:, hs] + kpe).astype(BF16)


def _prep_mixer_weights(w_in, w_uq, w_ukv):
    depth = w_in.shape[0]
    w_in, w_uq, w_ukv = (w.astype(BF16) for w in (w_in, w_uq, w_ukv))
    o2 = Q_LORA_RANK + KV_LORA_RANK
    o3 = o2 + MLA_ROPE_DIM
    zc = lambda n: jnp.zeros((depth, w_in.shape[1], n), BF16)
    win = jnp.concatenate([w_in[..., :o2], zc(MLA_NOPE_DIM), w_in[..., o2:o3], zc(32), w_in[..., o3:]], axis=-1)

    wq = w_uq.reshape(depth, Q_LORA_RANK, MLA_HEADS, MLA_QK_DIM)
    zq = jnp.zeros((depth, Q_LORA_RANK, MLA_HEADS, 32), BF16)
    wuq = jnp.concatenate([wq, zq], axis=-1).reshape(depth, Q_LORA_RANK, QPAD)

    wkv = w_ukv.reshape(depth, KV_LORA_RANK, MLA_HEADS, MLA_NOPE_DIM + MLA_V_DIM)
    wk = jnp.concatenate([wkv[..., :MLA_NOPE_DIM], jnp.zeros_like(wkv[..., :MLA_NOPE_DIM])], axis=-1)
    wv = wkv[..., MLA_NOPE_DIM:]
    wukv = jnp.concatenate([wk.reshape(depth, KV_LORA_RANK, QPAD), wv.reshape(depth, KV_LORA_RANK, MLA_WIDTH)],
                           axis=-1)
    return win, wuq, wukv


def _mixer_in(l, x2d, seq, gpre, win, gq, wuq, gkv, wukv, ct, st, tm=512):
    n = x2d.shape[0]
    b = n // seq
    tpb = seq // tm
    row = lambda w: pl.BlockSpec((tm, w), lambda i: (i, 0))
    full = lambda a: _layer_spec(a, l)
    outs = [(2 * QPAD + MLA_WIDTH, BF16), (3 * DIL_WIDTH, BF16)]
    strided_specs = [pl.BlockSpec((1, dil, tm // dil, 3 * DIL_WIDTH), lambda i: (i // tpb, 0, i % tpb, 0))
                     for dil in STRIDED_DILATIONS]
    strided_shapes = [jax.ShapeDtypeStruct((b, dil, seq // dil, 3 * DIL_WIDTH), BF16) for dil in STRIDED_DILATIONS]
    return pl.pallas_call(
        _mixer_in_kernel,
        grid=(n // tm,),
        in_specs=[row(D_MODEL), full(gpre), full(win), full(gq), full(wuq), full(gkv), full(wukv),
                  row(LANES), row(LANES)],
        out_specs=[row(w) for w, _ in outs] + strided_specs,
        out_shape=[jax.ShapeDtypeStruct((n, w), dt) for w, dt in outs] + strided_shapes,
        scratch_shapes=[pltpu.VMEM((tm // ROW_SUBTILE, DIL_SLABS // 2, ROW_SUBTILE, LANES), jnp.uint32),
                        pltpu.VMEM((tm // ROW_SUBTILE, DIL_SLABS // 2, 4, ROW_SUBTILE // 4, LANES), jnp.uint32)],
        compiler_params=_params("parallel"),
        name="mixer_in",
    )(x2d, gpre, win, gq, wuq, gkv, wukv, ct, st)


MLA_HEADS_PER_STEP = 4


def _mla_attn_kernel(q_ref, k_ref, v_ref, o_ref, s_scr, m_scr, r_scr, acc_scr, oe_scr, *, tq, nq, nh):
    lo = _low_half()

    def rows(i):
        return slice(i * tq, (i + 1) * tq)

    def block(qi, j):
        return qi * (qi + 1) // 2 + j

    def pieces(qi, j):
        if j < qi:
            return [(0, tq, tq)]
        return [(0, tq // 2, tq // 2), (tq // 2, tq // 2, tq)]

    def scores(h, qi, j):
        slot = h % 2
        hs = slice(h * LANES, (h + 1) * LANES)
        for r0, nr, nk in pieces(qi, j):
            rs = slice(r0, r0 + nr)
            s = _dot_nt(q_ref[0, qi * tq + r0:qi * tq + r0 + nr, hs], k_ref[0, j * tq:j * tq + nk, hs])
            if j == qi:
                row = lax.broadcasted_iota(jnp.int32, (nr, nk), 0) + r0
                col = lax.broadcasted_iota(jnp.int32, (nr, nk), 1)
                s = jnp.where(col <= row, s, NEG_BIG)
            s_scr[slot, block(qi, j), rs, :nk] = s
            m = s[:, :LANES] if j == 0 else jnp.maximum(m_scr[slot, qi, rs], s[:, :LANES])
            for t in range(1, nk // LANES):
                m = jnp.maximum(m, s[:, t * LANES:(t + 1) * LANES])
            if j == qi:
                r_scr[slot, qi, rs] = jnp.broadcast_to(jnp.max(m, axis=-1, keepdims=True), (nr, LANES))
            else:
                m_scr[slot, qi, rs] = m

    def weigh(h, qi, j):
        slot = h % 2
        pr = slice((h // 2) * LANES, (h // 2 + 1) * LANES)
        for r0, nr, nk in pieces(qi, j):
            rs = slice(r0, r0 + nr)
            vblk = v_ref[0, j * tq:j * tq + nk, pr]
            s = s_scr[slot, block(qi, j), rs, :nk]
            rmax = r_scr[slot, qi, rs]
            p = jnp.concatenate([jnp.exp2(s[:, t * LANES:(t + 1) * LANES] - rmax) for t in range(nk // LANES)],
                                axis=1)
            pv = _dot(p.astype(BF16), jnp.concatenate([vblk, jnp.ones_like(vblk)], axis=1))
            if j == 0:
                acc_scr[slot, qi, rs] = pv
            else:
                acc_scr[slot, qi, rs] += pv
        if j == qi:
            acc = acc_scr[slot, qi]
            o = acc[:, :LANES] / acc[:, LANES:]
            if h % 2 == 0:
                oe_scr[qi] = o
            else:
                o_ref[0, rows(qi), pr] = jnp.where(lo, oe_scr[qi], o).astype(o_ref.dtype)

    for stage in range(nh + 1):
        for qi in range(nq):
            for j in range(qi + 1):
                if stage < nh:
                    scores(stage, qi, j)
                if stage > 0:
                    weigh(stage - 1, qi, j)


def _mla_attention(qkv, tq=512, nh=MLA_HEADS_PER_STEP):
    b, s, _ = qkv.shape
    nq = s // tq
    groups = MLA_HEADS // nh
    seq = lambda w, first: pl.BlockSpec((1, s, w), lambda bi, hg: (bi, 0, first + hg))
    return pl.pallas_call(
        functools.partial(_mla_attn_kernel, tq=tq, nq=nq, nh=nh),
        grid=(b, MLA_HEADS // nh),
        in_specs=[seq(nh * LANES, 0), seq(nh * LANES, groups), seq(nh * HALF, 2 * QPAD // (nh * HALF))],
        out_specs=seq(nh * HALF, 0),
        out_shape=jax.ShapeDtypeStruct((b, s, MLA_WIDTH), BF16),
        scratch_shapes=[pltpu.VMEM((2, nq * (nq + 1) // 2, tq, tq), F32),
                        pltpu.VMEM((2, nq, tq, LANES), F32),
                        pltpu.VMEM((2, nq, tq, LANES), F32),
                        pltpu.VMEM((2, nq, tq, 2 * LANES), F32),
                        pltpu.VMEM((nq, tq, LANES), F32)],
        compiler_params=_params("parallel", "parallel"),
        name="mla_attn",
    )(qkv, qkv, qkv)


def _dil_pair(q, kk, vv, bias2, lo):
    zero = jnp.zeros_like(q)
    q2 = jnp.concatenate([jnp.where(lo, q, zero), jnp.where(lo, zero, q)], axis=0)
    s = _dot_nt(q2, kk) + bias2
    m = jnp.max(s, axis=-1, keepdims=True)
    e = jnp.exp2(s - m).astype(BF16)
    r = _dot(e, jnp.concatenate([vv, jnp.ones_like(vv)], axis=1))
    mb = jnp.broadcast_to(m, (2 * DIL_N, LANES))
    pick = lambda a: jnp.where(lo, a[:DIL_N], a[DIL_N:])
    return pick(r[:, :LANES]), pick(r[:, LANES:]), pick(mb)


def _dil_pair_first(q, kk, vv, bias2, lo):
    zk, zv = jnp.zeros_like(kk), jnp.zeros_like(vv)
    ones_lo = jnp.broadcast_to(jnp.where(lo, 1.0, 0.0), vv.shape).astype(BF16)
    ones_hi = jnp.broadcast_to(jnp.where(lo, 0.0, 1.0), vv.shape).astype(BF16)
    keys2 = jnp.concatenate([jnp.where(lo, kk, zk), jnp.where(lo, zk, kk)], axis=0)
    s = _dot_nt(q, keys2) + jnp.concatenate([bias2[:DIL_N, :DIL_N], bias2[DIL_N:, :DIL_N]], axis=1)
    m0 = jnp.max(s[:, :DIL_N], axis=-1, keepdims=True)
    m1 = jnp.max(s[:, DIL_N:], axis=-1, keepdims=True)
    e = jnp.concatenate([jnp.exp2(s[:, :DIL_N] - m0), jnp.exp2(s[:, DIL_N:] - m1)], axis=1).astype(BF16)
    w = jnp.concatenate([jnp.concatenate([jnp.where(lo, vv, zv), ones_lo], axis=1),
                         jnp.concatenate([jnp.where(lo, zv, vv), ones_hi], axis=1)], axis=0)
    r = _dot(e, w)
    return r[:, :LANES], r[:, LANES:], jnp.where(lo, m0, m1)


def _dilated_kernel(q_ref, k_ref, v_ref, q4_ref, k4_ref, v4_ref, q16_ref, k16_ref, v16_ref, bias_ref, o_ref,
                    *acc_scr, seq):
    lo = _low_half()
    acc = [acc_scr[3 * p:3 * p + 3] for p in range(len(DIL_PATTERNS))]
    strided = {4: (q4_ref, k4_ref, v4_ref), 16: (q16_ref, k16_ref, v16_ref)}

    def chunk(p, dil, u, first):
        r, c = (u, 0) if first else (u % dil, 1 + u // dil)
        row_q = c * DIL_N
        row_k, nk = (row_q, DIL_N) if first else (row_q - DIL_N, 2 * DIL_N)
        if dil == 1:
            q = q_ref[0, pl.ds(row_q, DIL_N), :]
            kk = k_ref[0, pl.ds(row_k, nk), :]
            vv = v_ref[0, pl.ds(row_k, nk), :]
        else:
            qr, kr, vr = strided[dil]
            q = qr[0, r, pl.ds(row_q, DIL_N), :]
            kk = kr[0, r, pl.ds(row_k, nk), :]
            vv = vr[0, r, pl.ds(row_k, nk), :]
        if first:
            res = _dil_pair_first(q, kk, vv, bias_ref[p, 0, 0], lo)
        else:
            res = _dil_pair(q, kk, vv, bias_ref[p, 1, 0], lo)
        t0 = r + dil * DIL_N * c
        out_rows = pl.ds(t0, DIL_N) if dil == 1 else pl.ds(t0, DIL_N, stride=dil)
        return out_rows, res

    def chunks(p, dil, us, first):
        results = [chunk(p, dil, u, first) for u in us]
        for out_rows, res in results:
            for dst, val in zip(acc[p], res):
                dst[out_rows, :] = val

    def merge(rs):
        ms = [a[2][rs, :] for a in acc]
        mx = functools.reduce(jnp.maximum, ms)
        fs = [jnp.exp2(m - mx) for m in ms]
        num = sum(f * a[0][rs, :] for f, a in zip(fs, acc))
        den = sum(f * a[1][rs, :] for f, a in zip(fs, acc))
        o_ref[0, rs, :] = (num / den).astype(o_ref.dtype)

    for p, (window, dil) in enumerate(DIL_PATTERNS):
        if dil > 1:
            chunks(p, dil, range(dil), True)
            if seq > window:
                chunks(p, dil, range(dil * (seq // window - 1)), False)
    (p1, (window1, _)), = [(p, pat) for p, pat in enumerate(DIL_PATTERNS) if pat[1] == 1]
    assert window1 == DIL_N
    group = 2
    for g0 in range(0, seq // DIL_N, group):
        if g0 == 0:
            chunks(p1, 1, [0], True)
            chunks(p1, 1, list(range(group - 1)), False)
        else:
            chunks(p1, 1, [c - 1 for c in range(g0, g0 + group)], False)
        merge(slice(g0 * DIL_N, (g0 + group) * DIL_N))


def _dilated_attention(nat, strided, bias):
    b, s, _ = nat.shape
    w = DIL_WIDTH
    npat = len(DIL_PATTERNS)
    tiles = w // LANES
    blk = pl.BlockSpec((1, s, LANES), lambda bi, hp: (bi, 0, hp))
    nat_specs = [pl.BlockSpec((1, s, LANES), lambda bi, hp, sec=sec: (bi, 0, sec * tiles + hp)) for sec in range(3)]
    strided_specs, strided_args = [], []
    for arr, dil in zip(strided, STRIDED_DILATIONS):
        for sec in range(3):
            strided_specs.append(pl.BlockSpec((1, dil, s // dil, LANES),
                                              lambda bi, hp, sec=sec: (bi, 0, 0, sec * tiles + hp)))
            strided_args.append(arr)
    return pl.pallas_call(
        functools.partial(_dilated_kernel, seq=s),
        grid=(b, DIL_HEADS // 2),
        in_specs=nat_specs + strided_specs
                 + [pl.BlockSpec((npat, 2, 1, 2 * DIL_N, 2 * DIL_N), lambda bi, hp: (0, 0, hp, 0, 0))],
        out_specs=blk,
        out_shape=jax.ShapeDtypeStruct((b, s, w), BF16),
        scratch_shapes=[pltpu.VMEM((s, LANES), F32)] * (3 * npat),
        compiler_params=_params("parallel", "parallel"),
        name="dilated",
    )(nat, nat, nat, *strided_args, bias)


def _mem_kv_kernel(mem_ref, g_ref, wk_ref, wv_ref, k_ref, v_ref):
    mn = _rms(mem_ref[0], g_ref[...]).astype(BF16)
    k_ref[0] = _dot(mn, wk_ref[...].astype(BF16)).astype(BF16)
    v_ref[0] = _dot(mn, wv_ref[...].astype(BF16)).astype(BF16)


def _mem_kv(l, mem, g, wk, wv):
    b, m, d = mem.shape
    full = lambda a: _layer_spec(a, l)
    blk = pl.BlockSpec((1, m, d), lambda i: (i, 0, 0))
    return pl.pallas_call(
        _mem_kv_kernel,
        grid=(b,),
        in_specs=[blk, full(g), full(wk), full(wv)],
        out_specs=[blk, blk],
        out_shape=[jax.ShapeDtypeStruct((b, m, d), BF16)] * 2,
        compiler_params=_params("parallel"),
        name="mem_kv",
    )(mem, g, wk, wv)


def _post_mixer_kernel(x_ref, om_ref, od_ref, wout_ref, gpm_ref, gpx_ref, wxq_ref, kx_ref, vx_ref,
                       wxo_ref, gpox_ref, wup_ref, wdn_ref, out_ref, wup_bf_ref, wdn_bf_ref):
    wup_bf_ref[...] = wup_ref[...].astype(BF16)
    wdn_bf_ref[...] = wdn_ref[...].astype(BF16)
    subs = [slice(c * ROW_SUBTILE, (c + 1) * ROW_SUBTILE) for c in range(x_ref.shape[1] // ROW_SUBTILE)]
    wout_a, wout_b = wout_ref[:MLA_WIDTH, :].astype(BF16), wout_ref[MLA_WIDTH:, :].astype(BF16)
    wxq, wxo = wxq_ref[...].astype(BF16), wxo_ref[...].astype(BF16)
    a = [_dot(om_ref[0, rs], wout_a) + _dot(od_ref[0, rs], wout_b) for rs in subs]
    x1 = [x_ref[0, rs] + _rms(ai, gpm_ref[...]) for rs, ai in zip(subs, a)]
    q = [(_dot(_rms(xi, gpx_ref[...]).astype(BF16), wxq) * (X_HEAD_DIM ** -0.5 * LOG2E)).astype(BF16)
         for xi in x1]
    o = []
    for qi in q:
        heads = []
        for h in range(X_HEADS):
            sl = slice(h * X_HEAD_DIM, (h + 1) * X_HEAD_DIM)
            s = _dot_nt(qi[:, sl], kx_ref[0, :, sl])
            m = jnp.max(s, axis=-1, keepdims=True)
            e = jnp.exp2(s - m)
            den = jnp.sum(e, axis=-1, keepdims=True)
            heads.append((_dot(e.astype(BF16), vx_ref[0, :, sl]) * (1.0 / den)).astype(BF16))
        o.append(jnp.concatenate(heads, axis=-1))
    cx = [_dot(oi, wxo) for oi in o]
    for rs, xi, ci in zip(subs, x1, cx):
        out_ref[0, rs] = xi + _rms(ci, gpox_ref[...])


def _post_mixer(l, x, om, od, wout, gpm, gpx, wxq, kx, vx, wxo, gpox, wup, wdn, tm=1024):
    b, s, d = x.shape
    steps = b * (s // tm)
    tok = lambda w: pl.BlockSpec((1, tm, w), lambda bi, i: (bi, i, 0))
    full = lambda a: _layer_spec(a, l)
    memb = pl.BlockSpec((1, MEM_LEN, d), lambda bi, i: (bi, 0, 0))
    band_in = lambda a: pl.BlockSpec((None, a.shape[1] // steps, a.shape[2]),
                                     lambda bi, i: (l, bi * (s // tm) + i, 0))
    band_out = lambda a: pl.BlockSpec((a.shape[1] // steps, a.shape[2]), lambda bi, i: (bi * (s // tm) + i, 0))
    return pl.pallas_call(
        _post_mixer_kernel,
        grid=(b, s // tm),
        in_specs=[tok(d), tok(MLA_WIDTH), tok(DIL_WIDTH),
                  full(wout), full(gpm), full(gpx), full(wxq), memb, memb, full(wxo), full(gpox),
                  band_in(wup), band_in(wdn)],
        out_specs=[tok(d), band_out(wup), band_out(wdn)],
        out_shape=[jax.ShapeDtypeStruct((b, s, d), F32),
                   jax.ShapeDtypeStruct(wup.shape[1:], BF16), jax.ShapeDtypeStruct(wdn.shape[1:], BF16)],
        compiler_params=_params("parallel", "parallel"),
        name="post_mixer",
    )(x, om, od, wout, gpm, gpx, wxq, kx, vx, wxo, gpox, wup, wdn)


def _mlp_kernel(x_ref, gpre_ref, wup_ref, wdn_ref, gpost_ref, out_ref):
    subs = [slice(c * ROW_SUBTILE, (c + 1) * ROW_SUBTILE) for c in range(x_ref.shape[0] // ROW_SUBTILE)]
    u = [jnp.maximum(_dot(_rms(x_ref[rs, :], gpre_ref[...]).astype(BF16), wup_ref[...]), 0.0) for rs in subs]
    y = [_dot((ui * ui).astype(BF16), wdn_ref[...]) for ui in u]
    for rs, yi in zip(subs, y):
        out_ref[rs, :] = x_ref[rs, :] + _rms(yi, gpost_ref[...])


def _mlp(l, x2d, gpre, wup, wdn, gpost, tm=512):
    n, d = x2d.shape
    row = pl.BlockSpec((tm, d), lambda i: (i, 0))
    full = lambda a: _layer_spec(a, l)
    whole = lambda a: pl.BlockSpec(a.shape, lambda i: (0, 0), pipeline_mode=pl.Buffered(1))
    return pl.pallas_call(
        _mlp_kernel,
        grid=(n // tm,),
        in_specs=[row, full(gpre), whole(wup), whole(wdn), full(gpost)],
        out_specs=row,
        out_shape=jax.ShapeDtypeStruct((n, d), F32),
        compiler_params=_params("parallel"),
        name="mlp",
    )(x2d, gpre, wup, wdn, gpost)


def kernel(x, mem, positions, rel_bias, w_in, q_norm_g, w_uq, kv_norm_g, w_ukv, w_mix_out, pre_mix_g, post_mix_g, w_xq, w_xk, w_xv, w_xo, mem_g, pre_xattn_g, post_xattn_g, w_up, w_down, pre_mlp_g, post_mlp_g):
    b, s, d = x.shape
    n = b * s
    ct, st = _rope_tables(positions)
    bias = _bias_tables(rel_bias).reshape(len(DIL_PATTERNS), 2, DIL_HEADS // 2, 2 * DIL_N, 2 * DIL_N)
    gain = lambda a: a.reshape(a.shape[0], 1, -1).astype(F32)
    gains = {k: gain(v) for k, v in dict(
        pre_mix=pre_mix_g, q_norm=q_norm_g, kv_norm=kv_norm_g, post_mix=post_mix_g, mem=mem_g,
        pre_x=pre_xattn_g, post_x=post_xattn_g, pre_mlp=pre_mlp_g, post_mlp=post_mlp_g).items()}
    win, wuq, wukv = _prep_mixer_weights(w_in, w_uq, w_ukv)
    for l in range(DEPTH):
        qkv, nat, *strided = _mixer_in(l, x.reshape(n, d), s, gains["pre_mix"], win, gains["q_norm"], wuq,
                                       gains["kv_norm"], wukv, ct, st)
        om = _mla_attention(qkv.reshape(b, s, -1))
        od = _dilated_attention(nat.reshape(b, s, -1), strided, bias)
        kx, vx = _mem_kv(l, mem, gains["mem"], w_xk, w_xv)
        x, wup, wdn = _post_mixer(l, x, om, od, w_mix_out, gains["post_mix"], gains["pre_x"], w_xq, kx, vx, w_xo,
                                  gains["post_x"], w_up, w_down)
        x = _mlp(l, x.reshape(n, d), gains["pre_mlp"], wup, wdn, gains["post_mlp"]).reshape(b, s, d)
    return x
```

```python
import functools
import math

import numpy as np
import jax
import jax.numpy as jnp
from jax import lax
from jax.experimental import pallas as pl
from jax.experimental.pallas import tpu as pltpu

D_MODEL = 1024
DEPTH = 2
MEM_LEN = 256
MLA_HEADS = 8
MLA_NOPE_DIM = 64
MLA_ROPE_DIM = 32
MLA_V_DIM = 64
Q_LORA_RANK = 384
KV_LORA_RANK = 256
ROPE_THETA = 10000.0
DIL_HEADS = 8
DIL_HEAD_DIM = 64
DIL_PATTERNS = ((128, 1), (512, 4), (2048, 16))
N_BUCKETS = 32
MAX_DISTANCE = 2048
X_HEADS = 4
X_HEAD_DIM = D_MODEL // X_HEADS
D_FF = 4 * D_MODEL
EPS = 1e-6

MLA_QK_DIM = MLA_NOPE_DIM + MLA_ROPE_DIM
DIL_WIDTH = DIL_HEADS * DIL_HEAD_DIM
MLA_WIDTH = MLA_HEADS * MLA_V_DIM
ROPE_HALF = MLA_ROPE_DIM // 2
DIL_N = 128
LANES = 128
HALF = LANES // 2
ROW_SUBTILE = 256
NEG_BIG = -1e30
LOG2E = math.log2(math.e)
VMEM_LIMIT_BYTES = 56 * 1024 * 1024

BF16 = jnp.bfloat16
F32 = jnp.float32


def _rms(x, g):
    return x * lax.rsqrt(jnp.mean(x * x, axis=-1, keepdims=True) + EPS) * g


def _dot(a, b):
    return jnp.dot(a, b, preferred_element_type=F32)


def _dot_nt(a, b):
    return lax.dot_general(a, b, (((1,), (1,)), ((), ())), preferred_element_type=F32)


def _params(*sem):
    return pltpu.CompilerParams(dimension_semantics=sem, vmem_limit_bytes=VMEM_LIMIT_BYTES)


def _layer_spec(a, l):
    return pl.BlockSpec((None,) + a.shape[1:], lambda *_: (l,) + (0,) * (a.ndim - 1),
                        pipeline_mode=pl.Buffered(1))


def _low_half():
    return lax.broadcasted_iota(jnp.int32, (1, LANES), 1) < HALF


TOKENS_PER_ROW = LANES // ROPE_HALF


def _rope_table_kernel(pos_ref, inv_ref, ct_ref, st_ref):
    tr = pos_ref.shape[0]
    ang = pos_ref[...] * inv_ref[...]
    cos = jnp.cos(ang)
    sin = jnp.sin(ang)
    lane = lax.broadcasted_iota(jnp.int32, (1, LANES), 1)
    x1 = jnp.logical_and(lane >= MLA_NOPE_DIM, lane < MLA_NOPE_DIM + ROPE_HALF)
    x2 = jnp.logical_and(lane >= MLA_NOPE_DIM + ROPE_HALF, lane < MLA_QK_DIM)
    for j in range(TOKENS_PER_ROW):
        to_x1 = (MLA_NOPE_DIM - ROPE_HALF * j) % LANES
        to_x2 = (MLA_NOPE_DIM + ROPE_HALF - ROPE_HALF * j) % LANES
        c1, c2 = pltpu.roll(cos, to_x1, axis=1), pltpu.roll(cos, to_x2, axis=1)
        s1, s2 = pltpu.roll(sin, to_x1, axis=1), pltpu.roll(sin, to_x2, axis=1)
        rows = pl.ds(j, tr, stride=TOKENS_PER_ROW)
        ct_ref[rows, :] = jnp.where(lane < MLA_NOPE_DIM, 1.0, jnp.where(x1, c1, jnp.where(x2, c2, 0.0)))
        st_ref[rows, :] = jnp.where(x1, -s1, jnp.where(x2, s2, 0.0))


def _rope_tables(positions):
    n = positions.size
    rows = n // TOKENS_PER_ROW
    pos_rep = jnp.repeat(positions.reshape(rows, TOKENS_PER_ROW).astype(F32), ROPE_HALF, axis=1)
    inv = ROPE_THETA ** (-jnp.arange(ROPE_HALF, dtype=F32) / ROPE_HALF)
    inv_rep = jnp.tile(inv, TOKENS_PER_ROW).reshape(1, LANES)
    tr = 256
    return pl.pallas_call(
        _rope_table_kernel,
        grid=(rows // tr,),
        in_specs=[pl.BlockSpec((tr, LANES), lambda i: (i, 0)),
                  pl.BlockSpec((1, LANES), lambda i: (0, 0))],
        out_specs=[pl.BlockSpec((tr * TOKENS_PER_ROW, LANES), lambda i: (i, 0))] * 2,
        out_shape=[jax.ShapeDtypeStruct((n, LANES), F32)] * 2,
        compiler_params=_params("parallel"),
        name="rope_tables",
    )(pos_rep, inv_rep)


def _bucket_tables():
    i = np.arange(DIL_N)[:, None]
    m = np.arange(2 * DIL_N)[None, :]
    rel = DIL_N + i - m
    band = (rel >= 0) & (rel <= DIL_N)
    max_exact = N_BUCKETS // 2
    out = []
    for _, dil in DIL_PATTERNS:
        d = np.maximum(rel * dil, 0)
        df = np.maximum(d.astype(np.float64), 1.0)
        large = max_exact + (np.log(df / max_exact) / math.log(MAX_DISTANCE / max_exact)
                             * (N_BUCKETS - max_exact)).astype(np.int32)
        large = np.minimum(large, N_BUCKETS - 1)
        bucket = np.where(d < max_exact, d, large)
        out.append(np.where(band, bucket, -1).astype(np.int32))
    return np.stack(out)


def _bias_table_kernel(rb_ref, bucket_ref, out_ref):
    bucket = bucket_ref[0]
    masked = jnp.full((DIL_N, DIL_N), NEG_BIG, F32)
    for h in range(DIL_HEADS):
        tab = jnp.full(bucket.shape, NEG_BIG, F32)
        for b in range(N_BUCKETS):
            tab = jnp.where(bucket == b, rb_ref[b, h] * LOG2E, tab)
        out_ref[0, 0, h] = jnp.concatenate([tab[:, DIL_N:], masked], axis=1)
        out_ref[0, 1, h] = tab


def _bias_tables(rel_bias):
    buckets = jnp.asarray(_bucket_tables())
    npat = len(DIL_PATTERNS)
    return pl.pallas_call(
        _bias_table_kernel,
        grid=(npat,),
        in_specs=[pl.BlockSpec(memory_space=pltpu.SMEM),
                  pl.BlockSpec((1, DIL_N, 2 * DIL_N), lambda p: (p, 0, 0))],
        out_specs=pl.BlockSpec((1, 2, DIL_HEADS, DIL_N, 2 * DIL_N), lambda p: (p, 0, 0, 0, 0)),
        out_shape=jax.ShapeDtypeStruct((npat, 2, DIL_HEADS, DIL_N, 2 * DIL_N), F32),
        compiler_params=_params("parallel"),
        name="bias_tables",
    )(rel_bias.astype(F32), buckets)


C_KR = Q_LORA_RANK + KV_LORA_RANK
C_QD = C_KR + LANES
QPAD = MLA_HEADS * LANES
DIL_SLABS = 3 * DIL_WIDTH // LANES
STRIDED_DILATIONS = tuple(d for _, d in DIL_PATTERNS if d > 1)


def _rotary(a, ct, st, x1_lanes):
    partner = jnp.where(x1_lanes, pltpu.roll(a, LANES - ROPE_HALF, axis=1), pltpu.roll(a, ROPE_HALF, axis=1))
    return a * ct + partner * st


def _mixer_in_kernel(x_ref, gpre_ref, wlat_ref, wdil_ref, gq_ref, wuq_ref, gkv_ref, wukv_ref, ct_ref, st_ref,
                     qkv_ref, nat_ref, d4_ref, d16_ref, slab_scr, cls_scr):
    lane = lax.broadcasted_iota(jnp.int32, (1, LANES), 1)
    x1_lanes = jnp.logical_and(lane >= MLA_NOPE_DIM, lane < MLA_NOPE_DIM + ROPE_HALF)
    nsec = DIL_SLABS // 3

    subs = [slice(c * ROW_SUBTILE, (c + 1) * ROW_SUBTILE) for c in range(x_ref.shape[0] // ROW_SUBTILE)]
    wl = wlat_ref[...]
    zeros = lambda n: jnp.zeros((wl.shape[0], n), BF16)
    w_lat = jnp.concatenate([wl[:, :C_KR], zeros(MLA_NOPE_DIM), wl[:, C_KR:], zeros(LANES - MLA_QK_DIM)], axis=1)
    xn = [_rms(x_ref[rs, :], gpre_ref[...]).astype(BF16) for rs in subs]
    zls = [_dot(xi, w_lat) for xi in xn]
    zds = [_dot(xi, wdil_ref[...]) for xi in xn]
    for c, (rs, z, zd) in enumerate(zip(subs, zls, zds)):
        ct = ct_ref[rs, :]
        st = st_ref[rs, :]

        per4, per16 = ROW_SUBTILE // 4, ROW_SUBTILE // 16
        for t2 in range(DIL_SLABS // 2):
            vals = []
            for t in (2 * t2, 2 * t2 + 1):
                val = zd[:, t * LANES:(t + 1) * LANES]
                if t < nsec:
                    val = val * (DIL_HEAD_DIM ** -0.5 * LOG2E)
                nat_ref[rs, t * LANES:(t + 1) * LANES] = val.astype(BF16)
                vals.append(val)
            slab_scr[c, t2] = pltpu.pack_elementwise(vals, packed_dtype=BF16)

            def emit(dref, r, rows, packed):
                for i in range(2):
                    tile = pltpu.unpack_elementwise(packed, index=i, packed_dtype=BF16, unpacked_dtype=F32)
                    dref[0, r, rows, (2 * t2 + i) * LANES:(2 * t2 + i + 1) * LANES] = tile.astype(BF16)

            for r4 in range(4):
                cls = slab_scr[c, t2, pl.ds(r4, per4, stride=4), :]
                cls_scr[c, t2, r4] = cls
                emit(d4_ref, r4, slice(c * per4, (c + 1) * per4), cls)
            for r4 in range(4):
                for hi in range(4):
                    emit(d16_ref, r4 + 4 * hi, slice(c * per16, (c + 1) * per16),
                         cls_scr[c, t2, r4, pl.ds(hi, per16, stride=4), :])

        cqn = _rms(z[:, :Q_LORA_RANK], gq_ref[...])
        qa = _dot(cqn.astype(BF16), wuq_ref[...])
        scale = MLA_QK_DIM ** -0.5 * LOG2E
        for h in range(MLA_HEADS):
            hs = slice(h * LANES, (h + 1) * LANES)
            qkv_ref[rs, hs] = (_rotary(qa[:, hs], ct, st, x1_lanes) * scale).astype(BF16)

        ckvn = _rms(z[:, Q_LORA_RANK:C_KR], gkv_ref[...])
        kv = _dot(ckvn.astype(BF16), wukv_ref[...])
        qkv_ref[rs, 2 * QPAD:] = kv[:, QPAD:].astype(BF16)
        kpe = _rotary(z[:, C_KR:C_QD], ct, st, x1_lanes)
        for h in range(MLA_HEADS):
            hs = slice(h * LANES, (h + 1) * LANES)
            qkv_ref[rs, QPAD + h * LANES:QPAD + (h + 1) * LANES] = (kv[:, hs] + kpe).astype(BF16)


def _prep_mixer_weights(w_in, w_uq, w_ukv):
    depth = w_in.shape[0]
    w_uq, w_ukv = w_uq.astype(BF16), w_ukv.astype(BF16)
    o3 = Q_LORA_RANK + KV_LORA_RANK + MLA_ROPE_DIM
    wlat, wdil = w_in[..., :o3].astype(BF16), w_in[..., o3:].astype(BF16)

    wq = w_uq.reshape(depth, Q_LORA_RANK, MLA_HEADS, MLA_QK_DIM)
    zq = jnp.zeros((depth, Q_LORA_RANK, MLA_HEADS, 32), BF16)
    wuq = jnp.concatenate([wq, zq], axis=-1).reshape(depth, Q_LORA_RANK, QPAD)

    wkv = w_ukv.reshape(depth, KV_LORA_RANK, MLA_HEADS, MLA_NOPE_DIM + MLA_V_DIM)
    wk = jnp.concatenate([wkv[..., :MLA_NOPE_DIM], jnp.zeros_like(wkv[..., :MLA_NOPE_DIM])], axis=-1)
    wv = wkv[..., MLA_NOPE_DIM:]
    wukv = jnp.concatenate([wk.reshape(depth, KV_LORA_RANK, QPAD), wv.reshape(depth, KV_LORA_RANK, MLA_WIDTH)],
                           axis=-1)
    return wlat, wdil, wuq, wukv


def _mixer_in(l, x2d, seq, gpre, wlat, wdil, gq, wuq, gkv, wukv, ct, st, tm=512):
    n = x2d.shape[0]
    b = n // seq
    tpb = seq // tm
    row = lambda w: pl.BlockSpec((tm, w), lambda i: (i, 0))
    full = lambda a: _layer_spec(a, l)
    outs = [(2 * QPAD + MLA_WIDTH, BF16), (3 * DIL_WIDTH, BF16)]
    strided_specs = [pl.BlockSpec((1, dil, tm // dil, 3 * DIL_WIDTH), lambda i: (i // tpb, 0, i % tpb, 0))
                     for dil in STRIDED_DILATIONS]
    strided_shapes = [jax.ShapeDtypeStruct((b, dil, seq // dil, 3 * DIL_WIDTH), BF16) for dil in STRIDED_DILATIONS]
    return pl.pallas_call(
        _mixer_in_kernel,
        grid=(n // tm,),
        in_specs=[row(D_MODEL), full(gpre), full(wlat), full(wdil), full(gq), full(wuq), full(gkv), full(wukv),
                  row(LANES), row(LANES)],
        out_specs=[row(w) for w, _ in outs] + strided_specs,
        out_shape=[jax.ShapeDtypeStruct((n, w), dt) for w, dt in outs] + strided_shapes,
        scratch_shapes=[pltpu.VMEM((tm // ROW_SUBTILE, DIL_SLABS // 2, ROW_SUBTILE, LANES), jnp.uint32),
                        pltpu.VMEM((tm // ROW_SUBTILE, DIL_SLABS // 2, 4, ROW_SUBTILE // 4, LANES), jnp.uint32)],
        compiler_params=_params("parallel"),
        name="mixer_in",
    )(x2d, gpre, wlat, wdil, gq, wuq, gkv, wukv, ct, st)


MLA_HEADS_PER_STEP = 4


def _mla_attn_kernel(q_ref, k_ref, v_ref, o_ref, s_scr, m_scr, r_scr, acc_scr, oe_scr, *, tq, nq, nh):
    lo = _low_half()

    def rows(i):
        return slice(i * tq, (i + 1) * tq)

    def block(qi, j):
        return qi * (qi + 1) // 2 + j

    def pieces(qi, j):
        if j < qi:
            return [(0, tq, tq)]
        return [(0, tq // 2, tq // 2), (tq // 2, tq // 2, tq)]

    def scores(h, qi, j):
        slot = h % 2
        hs = slice(h * LANES, (h + 1) * LANES)
        for r0, nr, nk in pieces(qi, j):
            rs = slice(r0, r0 + nr)
            s = _dot_nt(q_ref[0, qi * tq + r0:qi * tq + r0 + nr, hs], k_ref[0, j * tq:j * tq + nk, hs])
            if j == qi:
                row = lax.broadcasted_iota(jnp.int32, (nr, nk), 0) + r0
                col = lax.broadcasted_iota(jnp.int32, (nr, nk), 1)
                s = jnp.where(col <= row, s, NEG_BIG)
            s_scr[slot, block(qi, j), rs, :nk] = s
            m = s[:, :LANES] if j == 0 else jnp.maximum(m_scr[slot, qi, rs], s[:, :LANES])
            for t in range(1, nk // LANES):
                m = jnp.maximum(m, s[:, t * LANES:(t + 1) * LANES])
            if j == qi:
                r_scr[slot, qi, rs] = jnp.broadcast_to(jnp.max(m, axis=-1, keepdims=True), (nr, LANES))
            else:
                m_scr[slot, qi, rs] = m

    def weigh(h, qi, j):
        slot = h % 2
        pr = slice((h // 2) * LANES, (h // 2 + 1) * LANES)
        for r0, nr, nk in pieces(qi, j):
            rs = slice(r0, r0 + nr)
            vblk = v_ref[0, j * tq:j * tq + nk, pr]
            s = s_scr[slot, block(qi, j), rs, :nk]
            rmax = r_scr[slot, qi, rs]
            p = jnp.concatenate([jnp.exp2(s[:, t * LANES:(t + 1) * LANES] - rmax) for t in range(nk // LANES)],
                                axis=1)
            pv = _dot(p.astype(BF16), jnp.concatenate([vblk, jnp.ones_like(vblk)], axis=1))
            if j == 0:
                acc_scr[slot, qi, rs] = pv
            else:
                acc_scr[slot, qi, rs] += pv
        if j == qi:
            acc = acc_scr[slot, qi]
            o = acc[:, :LANES] / acc[:, LANES:]
            if h % 2 == 0:
                oe_scr[qi] = o
            else:
                o_ref[0, rows(qi), pr] = jnp.where(lo, oe_scr[qi], o).astype(o_ref.dtype)

    for stage in range(nh + 1):
        for qi in range(nq):
            for j in range(qi + 1):
                if stage < nh:
                    scores(stage, qi, j)
                if stage > 0:
                    weigh(stage - 1, qi, j)


def _mla_attention(qkv, tq=512, nh=MLA_HEADS_PER_STEP):
    b, s, _ = qkv.shape
    nq = s // tq
    groups = MLA_HEADS // nh
    seq = lambda w, first: pl.BlockSpec((1, s, w), lambda bi, hg: (bi, 0, first + hg))
    return pl.pallas_call(
        functools.partial(_mla_attn_kernel, tq=tq, nq=nq, nh=nh),
        grid=(b, MLA_HEADS // nh),
        in_specs=[seq(nh * LANES, 0), seq(nh * LANES, groups), seq(nh * HALF, 2 * QPAD // (nh * HALF))],
        out_specs=seq(nh * HALF, 0),
        out_shape=jax.ShapeDtypeStruct((b, s, MLA_WIDTH), BF16),
        scratch_shapes=[pltpu.VMEM((2, nq * (nq + 1) // 2, tq, tq), F32),
                        pltpu.VMEM((2, nq, tq, LANES), F32),
                        pltpu.VMEM((2, nq, tq, LANES), F32),
                        pltpu.VMEM((2, nq, tq, 2 * LANES), F32),
                        pltpu.VMEM((nq, tq, LANES), F32)],
        compiler_params=_params("parallel", "parallel"),
        name="mla_attn",
    )(qkv, qkv, qkv)


def _dil_pair(q, kk, vv, bias2, lo):
    zero = jnp.zeros_like(q)
    q2 = jnp.concatenate([jnp.where(lo, q, zero), jnp.where(lo, zero, q)], axis=0)
    s = _dot_nt(q2, kk) + bias2
    m = jnp.max(s, axis=-1, keepdims=True)
    e = jnp.exp2(s - m).astype(BF16)
    r = _dot(e, jnp.concatenate([vv, jnp.ones_like(vv)], axis=1))
    mb = jnp.broadcast_to(m, (2 * DIL_N, LANES))
    pick = lambda a: jnp.where(lo, a[:DIL_N], a[DIL_N:])
    return pick(r[:, :LANES]), pick(r[:, LANES:]), pick(mb)


def _dil_pair_first(q, kk, vv, bias2, lo):
    zk, zv = jnp.zeros_like(kk), jnp.zeros_like(vv)
    ones_lo = jnp.broadcast_to(jnp.where(lo, 1.0, 0.0), vv.shape).astype(BF16)
    ones_hi = jnp.broadcast_to(jnp.where(lo, 0.0, 1.0), vv.shape).astype(BF16)
    keys2 = jnp.concatenate([jnp.where(lo, kk, zk), jnp.where(lo, zk, kk)], axis=0)
    s = _dot_nt(q, keys2) + jnp.concatenate([bias2[:DIL_N, :DIL_N], bias2[DIL_N:, :DIL_N]], axis=1)
    m0 = jnp.max(s[:, :DIL_N], axis=-1, keepdims=True)
    m1 = jnp.max(s[:, DIL_N:], axis=-1, keepdims=True)
    e = jnp.concatenate([jnp.exp2(s[:, :DIL_N] - m0), jnp.exp2(s[:, DIL_N:] - m1)], axis=1).astype(BF16)
    w = jnp.concatenate([jnp.concatenate([jnp.where(lo, vv, zv), ones_lo], axis=1),
                         jnp.concatenate([jnp.where(lo, zv, vv), ones_hi], axis=1)], axis=0)
    r = _dot(e, w)
    return r[:, :LANES], r[:, LANES:], jnp.where(lo, m0, m1)


def _dilated_kernel(q_ref, k_ref, v_ref, q4_ref, k4_ref, v4_ref, q16_ref, k16_ref, v16_ref, bias_ref, o_ref,
                    *acc_scr, seq):
    lo = _low_half()
    acc = [acc_scr[3 * p:3 * p + 3] for p in range(len(DIL_PATTERNS))]
    strided = {4: (q4_ref, k4_ref, v4_ref), 16: (q16_ref, k16_ref, v16_ref)}

    def chunk(p, dil, u, first):
        r, c = (u, 0) if first else (u % dil, 1 + u // dil)
        row_q = c * DIL_N
        row_k, nk = (row_q, DIL_N) if first else (row_q - DIL_N, 2 * DIL_N)
        if dil == 1:
            q = q_ref[0, pl.ds(row_q, DIL_N), :]
            kk = k_ref[0, pl.ds(row_k, nk), :]
            vv = v_ref[0, pl.ds(row_k, nk), :]
        else:
            qr, kr, vr = strided[dil]
            q = qr[0, r, pl.ds(row_q, DIL_N), :]
            kk = kr[0, r, pl.ds(row_k, nk), :]
            vv = vr[0, r, pl.ds(row_k, nk), :]
        if first:
            res = _dil_pair_first(q, kk, vv, bias_ref[p, 0, 0], lo)
        else:
            res = _dil_pair(q, kk, vv, bias_ref[p, 1, 0], lo)
        t0 = r + dil * DIL_N * c
        out_rows = pl.ds(t0, DIL_N) if dil == 1 else pl.ds(t0, DIL_N, stride=dil)
        return out_rows, res

    def chunks(p, dil, us, first):
        results = [chunk(p, dil, u, first) for u in us]
        for out_rows, res in results:
            for dst, val in zip(acc[p], res):
                dst[out_rows, :] = val

    def merge(rs):
        ms = [a[2][rs, :] for a in acc]
        mx = functools.reduce(jnp.maximum, ms)
        fs = [jnp.exp2(m - mx) for m in ms]
        num = sum(f * a[0][rs, :] for f, a in zip(fs, acc))
        den = sum(f * a[1][rs, :] for f, a in zip(fs, acc))
        o_ref[0, rs, :] = (num / den).astype(o_ref.dtype)

    for p, (window, dil) in enumerate(DIL_PATTERNS):
        if dil > 1:
            chunks(p, dil, range(dil), True)
            if seq > window:
                chunks(p, dil, range(dil * (seq // window - 1)), False)
    (p1, (window1, _)), = [(p, pat) for p, pat in enumerate(DIL_PATTERNS) if pat[1] == 1]
    assert window1 == DIL_N
    group = 2
    for g0 in range(0, seq // DIL_N, group):
        if g0 == 0:
            chunks(p1, 1, [0], True)
            chunks(p1, 1, list(range(group - 1)), False)
        else:
            chunks(p1, 1, [c - 1 for c in range(g0, g0 + group)], False)
        merge(slice(g0 * DIL_N, (g0 + group) * DIL_N))


def _dilated_attention(nat, strided, bias):
    b, s, _ = nat.shape
    w = DIL_WIDTH
    npat = len(DIL_PATTERNS)
    tiles = w // LANES
    blk = pl.BlockSpec((1, s, LANES), lambda bi, hp: (bi, 0, hp))
    nat_specs = [pl.BlockSpec((1, s, LANES), lambda bi, hp, sec=sec: (bi, 0, sec * tiles + hp)) for sec in range(3)]
    strided_specs, strided_args = [], []
    for arr, dil in zip(strided, STRIDED_DILATIONS):
        for sec in range(3):
            strided_specs.append(pl.BlockSpec((1, dil, s // dil, LANES),
                                              lambda bi, hp, sec=sec: (bi, 0, 0, sec * tiles + hp)))
            strided_args.append(arr)
    return pl.pallas_call(
        functools.partial(_dilated_kernel, seq=s),
        grid=(b, DIL_HEADS // 2),
        in_specs=nat_specs + strided_specs
                 + [pl.BlockSpec((npat, 2, 1, 2 * DIL_N, 2 * DIL_N), lambda bi, hp: (0, 0, hp, 0, 0))],
        out_specs=blk,
        out_shape=jax.ShapeDtypeStruct((b, s, w), BF16),
        scratch_shapes=[pltpu.VMEM((s, LANES), F32)] * (3 * npat),
        compiler_params=_params("parallel", "parallel"),
        name="dilated",
    )(nat, nat, nat, *strided_args, bias)


def _mem_kv_kernel(mem_ref, g_ref, wk_ref, wv_ref, k_ref, v_ref):
    mn = _rms(mem_ref[0], g_ref[...]).astype(BF16)
    k_ref[0] = _dot(mn, wk_ref[...].astype(BF16)).astype(BF16)
    v_ref[0] = _dot(mn, wv_ref[...].astype(BF16)).astype(BF16)


def _mem_kv(l, mem, g, wk, wv):
    b, m, d = mem.shape
    full = lambda a: _layer_spec(a, l)
    blk = pl.BlockSpec((1, m, d), lambda i: (i, 0, 0))
    return pl.pallas_call(
        _mem_kv_kernel,
        grid=(b,),
        in_specs=[blk, full(g), full(wk), full(wv)],
        out_specs=[blk, blk],
        out_shape=[jax.ShapeDtypeStruct((b, m, d), BF16)] * 2,
        compiler_params=_params("parallel"),
        name="mem_kv",
    )(mem, g, wk, wv)


def _post_mixer_kernel(x_ref, om_ref, od_ref, wout_ref, gpm_ref, gpx_ref, wxq_ref, kx_ref, vx_ref,
                       wxo_ref, gpox_ref, wup_ref, wdn_ref, out_ref, wup_bf_ref, wdn_bf_ref):
    wup_bf_ref[...] = wup_ref[...].astype(BF16)
    wdn_bf_ref[...] = wdn_ref[...].astype(BF16)
    subs = [slice(c * ROW_SUBTILE, (c + 1) * ROW_SUBTILE) for c in range(x_ref.shape[1] // ROW_SUBTILE)]
    wout_a, wout_b = wout_ref[:MLA_WIDTH, :].astype(BF16), wout_ref[MLA_WIDTH:, :].astype(BF16)
    wxq, wxo = wxq_ref[...].astype(BF16), wxo_ref[...].astype(BF16)
    a = [_dot(om_ref[0, rs], wout_a) + _dot(od_ref[0, rs], wout_b) for rs in subs]
    x1 = [x_ref[0, rs] + _rms(ai, gpm_ref[...]) for rs, ai in zip(subs, a)]
    q = [(_dot(_rms(xi, gpx_ref[...]).astype(BF16), wxq) * (X_HEAD_DIM ** -0.5 * LOG2E)).astype(BF16)
         for xi in x1]
    o = []
    for qi in q:
        heads = []
        for h in range(X_HEADS):
            sl = slice(h * X_HEAD_DIM, (h + 1) * X_HEAD_DIM)
            s = _dot_nt(qi[:, sl], kx_ref[0, :, sl])
            m = jnp.max(s, axis=-1, keepdims=True)
            e = jnp.exp2(s - m)
            den = jnp.sum(e, axis=-1, keepdims=True)
            heads.append((_dot(e.astype(BF16), vx_ref[0, :, sl]) * (1.0 / den)).astype(BF16))
        o.append(jnp.concatenate(heads, axis=-1))
    cx = [_dot(oi, wxo) for oi in o]
    for rs, xi, ci in zip(subs, x1, cx):
        out_ref[0, rs] = xi + _rms(ci, gpox_ref[...])


def _post_mixer(l, x, om, od, wout, gpm, gpx, wxq, kx, vx, wxo, gpox, wup, wdn, tm=1024):
    b, s, d = x.shape
    steps = b * (s // tm)
    tok = lambda w: pl.BlockSpec((1, tm, w), lambda bi, i: (bi, i, 0))
    full = lambda a: _layer_spec(a, l)
    memb = pl.BlockSpec((1, MEM_LEN, d), lambda bi, i: (bi, 0, 0))
    band_in = lambda a: pl.BlockSpec((None, a.shape[1] // steps, a.shape[2]),
                                     lambda bi, i: (l, bi * (s // tm) + i, 0))
    band_out = lambda a: pl.BlockSpec((a.shape[1] // steps, a.shape[2]), lambda bi, i: (bi * (s // tm) + i, 0))
    return pl.pallas_call(
        _post_mixer_kernel,
        grid=(b, s // tm),
        in_specs=[tok(d), tok(MLA_WIDTH), tok(DIL_WIDTH),
                  full(wout), full(gpm), full(gpx), full(wxq), memb, memb, full(wxo), full(gpox),
                  band_in(wup), band_in(wdn)],
        out_specs=[tok(d), band_out(wup), band_out(wdn)],
        out_shape=[jax.ShapeDtypeStruct((b, s, d), F32),
                   jax.ShapeDtypeStruct(wup.shape[1:], BF16), jax.ShapeDtypeStruct(wdn.shape[1:], BF16)],
        compiler_params=_params("parallel", "parallel"),
        name="post_mixer",
    )(x, om, od, wout, gpm, gpx, wxq, kx, vx, wxo, gpox, wup, wdn)


def _mlp_kernel(x_ref, gpre_ref, wup_ref, wdn_ref, gpost_ref, out_ref):
    subs = [slice(c * ROW_SUBTILE, (c + 1) * ROW_SUBTILE) for c in range(x_ref.shape[0] // ROW_SUBTILE)]
    u = [jnp.maximum(_dot(_rms(x_ref[rs, :], gpre_ref[...]).astype(BF16), wup_ref[...]), 0.0) for rs in subs]
    y = [_dot((ui * ui).astype(BF16), wdn_ref[...]) for ui in u]
    for rs, yi in zip(subs, y):
        out_ref[rs, :] = x_ref[rs, :] + _rms(yi, gpost_ref[...])


def _mlp(l, x2d, gpre, wup, wdn, gpost, tm=512):
    n, d = x2d.shape
    row = pl.BlockSpec((tm, d), lambda i: (i, 0))
    full = lambda a: _layer_spec(a, l)
    whole = lambda a: pl.BlockSpec(a.shape, lambda i: (0, 0), pipeline_mode=pl.Buffered(1))
    return pl.pallas_call(
        _mlp_kernel,
        grid=(n // tm,),
        in_specs=[row, full(gpre), whole(wup), whole(wdn), full(gpost)],
        out_specs=row,
        out_shape=jax.ShapeDtypeStruct((n, d), F32),
        compiler_params=_params("parallel"),
        name="mlp",
    )(x2d, gpre, wup, wdn, gpost)


def kernel(x, mem, positions, rel_bias, w_in, q_norm_g, w_uq, kv_norm_g, w_ukv, w_mix_out, pre_mix_g, post_mix_g, w_xq, w_xk, w_xv, w_xo, mem_g, pre_xattn_g, post_xattn_g, w_up, w_down, pre_mlp_g, post_mlp_g):
    b, s, d = x.shape
    n = b * s
    ct, st = _rope_tables(positions)
    bias = _bias_tables(rel_bias).reshape(len(DIL_PATTERNS), 2, DIL_HEADS // 2, 2 * DIL_N, 2 * DIL_N)
    gain = lambda a: a.reshape(a.shape[0], 1, -1).astype(F32)
    gains = {k: gain(v) for k, v in dict(
        pre_mix=pre_mix_g, q_norm=q_norm_g, kv_norm=kv_norm_g, post_mix=post_mix_g, mem=mem_g,
        pre_x=pre_xattn_g, post_x=post_xattn_g, pre_mlp=pre_mlp_g, post_mlp=post_mlp_g).items()}
    wlat, wdil, wuq, wukv = _prep_mixer_weights(w_in, w_uq, w_ukv)
    for l in range(DEPTH):
        qkv, nat, *strided = _mixer_in(l, x.reshape(n, d), s, gains["pre_mix"], wlat, wdil, gains["q_norm"], wuq,
                                       gains["kv_norm"], wukv, ct, st)
        om = _mla_attention(qkv.reshape(b, s, -1))
        od = _dilated_attention(nat.reshape(b, s, -1), strided, bias)
        kx, vx = _mem_kv(l, mem, gains["mem"], w_xk, w_xv)
        x, wup, wdn = _post_mixer(l, x, om, od, w_mix_out, gains["post_mix"], gains["pre_x"], w_xq, kx, vx, w_xo,
                                  gains["post_x"], w_up, w_down)
        x = _mlp(l, x.reshape(n, d), gains["pre_mlp"], wup, wdn, gains["post_mlp"]).reshape(b, s, d)
    return x
```

```python
import functools
import math

import numpy as np
import jax
import jax.numpy as jnp
from jax import lax
from jax.experimental import pallas as pl
from jax.experimental.pallas import tpu as pltpu

D_MODEL = 1024
DEPTH = 2
MEM_LEN = 256
MLA_HEADS = 8
MLA_NOPE_DIM = 64
MLA_ROPE_DIM = 32
MLA_V_DIM = 64
Q_LORA_RANK = 384
KV_LORA_RANK = 256
ROPE_THETA = 10000.0
DIL_HEADS = 8
DIL_HEAD_DIM = 64
DIL_PATTERNS = ((128, 1), (512, 4), (2048, 16))
N_BUCKETS = 32
MAX_DISTANCE = 2048
X_HEADS = 4
X_HEAD_DIM = D_MODEL // X_HEADS
D_FF = 4 * D_MODEL
EPS = 1e-6

MLA_QK_DIM = MLA_NOPE_DIM + MLA_ROPE_DIM
DIL_WIDTH = DIL_HEADS * DIL_HEAD_DIM
MLA_WIDTH = MLA_HEADS * MLA_V_DIM
ROPE_HALF = MLA_ROPE_DIM // 2
DIL_N = 128
LANES = 128
HALF = LANES // 2
ROW_SUBTILE = 256
NEG_BIG = -1e30
LOG2E = math.log2(math.e)
VMEM_LIMIT_BYTES = 56 * 1024 * 1024

BF16 = jnp.bfloat16
F32 = jnp.float32


def _rms(x, g):
    return x * lax.rsqrt(jnp.mean(x * x, axis=-1, keepdims=True) + EPS) * g


def _dot(a, b):
    return jnp.dot(a, b, preferred_element_type=F32)


def _dot_nt(a, b):
    return lax.dot_general(a, b, (((1,), (1,)), ((), ())), preferred_element_type=F32)


def _params(*sem):
    return pltpu.CompilerParams(dimension_semantics=sem, vmem_limit_bytes=VMEM_LIMIT_BYTES)


def _layer_spec(a, l):
    return pl.BlockSpec((None,) + a.shape[1:], lambda *_: (l,) + (0,) * (a.ndim - 1),
                        pipeline_mode=pl.Buffered(1))


def _low_half():
    return lax.broadcasted_iota(jnp.int32, (1, LANES), 1) < HALF


TOKENS_PER_ROW = LANES // ROPE_HALF


def _rope_table_kernel(pos_ref, inv_ref, ct_ref, st_ref):
    tr = pos_ref.shape[0]
    ang = pos_ref[...] * inv_ref[...]
    cos = jnp.cos(ang)
    sin = jnp.sin(ang)
    lane = lax.broadcasted_iota(jnp.int32, (1, LANES), 1)
    x1 = jnp.logical_and(lane >= MLA_NOPE_DIM, lane < MLA_NOPE_DIM + ROPE_HALF)
    x2 = jnp.logical_and(lane >= MLA_NOPE_DIM + ROPE_HALF, lane < MLA_QK_DIM)
    for j in range(TOKENS_PER_ROW):
        to_x1 = (MLA_NOPE_DIM - ROPE_HALF * j) % LANES
        to_x2 = (MLA_NOPE_DIM + ROPE_HALF - ROPE_HALF * j) % LANES
        c1, c2 = pltpu.roll(cos, to_x1, axis=1), pltpu.roll(cos, to_x2, axis=1)
        s1, s2 = pltpu.roll(sin, to_x1, axis=1), pltpu.roll(sin, to_x2, axis=1)
        rows = pl.ds(j, tr, stride=TOKENS_PER_ROW)
        ct_ref[rows, :] = jnp.where(lane < MLA_NOPE_DIM, 1.0, jnp.where(x1, c1, jnp.where(x2, c2, 0.0)))
        st_ref[rows, :] = jnp.where(x1, -s1, jnp.where(x2, s2, 0.0))


def _rope_tables(positions):
    n = positions.size
    rows = n // TOKENS_PER_ROW
    pos_rep = jnp.repeat(positions.reshape(rows, TOKENS_PER_ROW).astype(F32), ROPE_HALF, axis=1)
    inv = ROPE_THETA ** (-jnp.arange(ROPE_HALF, dtype=F32) / ROPE_HALF)
    inv_rep = jnp.tile(inv, TOKENS_PER_ROW).reshape(1, LANES)
    tr = 256
    return pl.pallas_call(
        _rope_table_kernel,
        grid=(rows // tr,),
        in_specs=[pl.BlockSpec((tr, LANES), lambda i: (i, 0)),
                  pl.BlockSpec((1, LANES), lambda i: (0, 0))],
        out_specs=[pl.BlockSpec((tr * TOKENS_PER_ROW, LANES), lambda i: (i, 0))] * 2,
        out_shape=[jax.ShapeDtypeStruct((n, LANES), F32)] * 2,
        compiler_params=_params("parallel"),
        name="rope_tables",
    )(pos_rep, inv_rep)


def _bucket_tables():
    i = np.arange(DIL_N)[:, None]
    m = np.arange(2 * DIL_N)[None, :]
    rel = DIL_N + i - m
    band = (rel >= 0) & (rel <= DIL_N)
    max_exact = N_BUCKETS // 2
    out = []
    for _, dil in DIL_PATTERNS:
        d = np.maximum(rel * dil, 0)
        df = np.maximum(d.astype(np.float64), 1.0)
        large = max_exact + (np.log(df / max_exact) / math.log(MAX_DISTANCE / max_exact)
                             * (N_BUCKETS - max_exact)).astype(np.int32)
        large = np.minimum(large, N_BUCKETS - 1)
        bucket = np.where(d < max_exact, d, large)
        out.append(np.where(band, bucket, -1).astype(np.int32))
    return np.stack(out)


def _bias_table_kernel(rb_ref, bucket_ref, out_ref):
    bucket = bucket_ref[0]
    masked = jnp.full((DIL_N, DIL_N), NEG_BIG, F32)
    for h in range(DIL_HEADS):
        tab = jnp.full(bucket.shape, NEG_BIG, F32)
        for b in range(N_BUCKETS):
            tab = jnp.where(bucket == b, rb_ref[b, h] * LOG2E, tab)
        out_ref[0, 0, h] = jnp.concatenate([tab[:, DIL_N:], masked], axis=1)
        out_ref[0, 1, h] = tab


def _bias_tables(rel_bias):
    buckets = jnp.asarray(_bucket_tables())
    npat = len(DIL_PATTERNS)
    return pl.pallas_call(
        _bias_table_kernel,
        grid=(npat,),
        in_specs=[pl.BlockSpec(memory_space=pltpu.SMEM),
                  pl.BlockSpec((1, DIL_N, 2 * DIL_N), lambda p: (p, 0, 0))],
        out_specs=pl.BlockSpec((1, 2, DIL_HEADS, DIL_N, 2 * DIL_N), lambda p: (p, 0, 0, 0, 0)),
        out_shape=jax.ShapeDtypeStruct((npat, 2, DIL_HEADS, DIL_N, 2 * DIL_N), F32),
        compiler_params=_params("parallel"),
        name="bias_tables",
    )(rel_bias.astype(F32), buckets)


C_KR = Q_LORA_RANK + KV_LORA_RANK
C_QD = C_KR + LANES
QPAD = MLA_HEADS * LANES
DIL_SLABS = 3 * DIL_WIDTH // LANES
STRIDED_DILATIONS = tuple(d for _, d in DIL_PATTERNS if d > 1)


def _rotary(a, ct, st, x1_lanes):
    partner = jnp.where(x1_lanes, pltpu.roll(a, LANES - ROPE_HALF, axis=1), pltpu.roll(a, ROPE_HALF, axis=1))
    return a * ct + partner * st


def _mixer_in_kernel(x_ref, gpre_ref, wlat_ref, wdil_ref, gq_ref, wuq_ref, gkv_ref, wukv_ref, ct_ref, st_ref,
                     qkv_ref, nat_ref, d4_ref, d16_ref, slab_scr, cls_scr):
    lane = lax.broadcasted_iota(jnp.int32, (1, LANES), 1)
    x1_lanes = jnp.logical_and(lane >= MLA_NOPE_DIM, lane < MLA_NOPE_DIM + ROPE_HALF)
    nsec = DIL_SLABS // 3

    subs = [slice(c * ROW_SUBTILE, (c + 1) * ROW_SUBTILE) for c in range(x_ref.shape[0] // ROW_SUBTILE)]
    wl = wlat_ref[...]
    zeros = lambda n: jnp.zeros((wl.shape[0], n), BF16)
    w_lat = jnp.concatenate([wl[:, :C_KR], zeros(MLA_NOPE_DIM), wl[:, C_KR:], zeros(LANES - MLA_QK_DIM)], axis=1)
    xn = [_rms(x_ref[rs, :], gpre_ref[...]).astype(BF16) for rs in subs]
    zls = [_dot(xi, w_lat) for xi in xn]
    zds = [_dot(xi, wdil_ref[...]) for xi in xn]
    for c, (rs, z, zd) in enumerate(zip(subs, zls, zds)):
        ct = ct_ref[rs, :]
        st = st_ref[rs, :]

        per4, per16 = ROW_SUBTILE // 4, ROW_SUBTILE // 16
        for t2 in range(DIL_SLABS // 2):
            vals = []
            for t in (2 * t2, 2 * t2 + 1):
                val = zd[:, t * LANES:(t + 1) * LANES]
                if t < nsec:
                    val = val * (DIL_HEAD_DIM ** -0.5 * LOG2E)
                nat_ref[rs, t * LANES:(t + 1) * LANES] = val.astype(BF16)
                vals.append(val)
            slab_scr[c, t2] = pltpu.pack_elementwise(vals, packed_dtype=BF16)

            def emit(dref, r, rows, packed):
                for i in range(2):
                    tile = pltpu.unpack_elementwise(packed, index=i, packed_dtype=BF16, unpacked_dtype=F32)
                    dref[0, r, rows, (2 * t2 + i) * LANES:(2 * t2 + i + 1) * LANES] = tile.astype(BF16)

            for r4 in range(4):
                cls = slab_scr[c, t2, pl.ds(r4, per4, stride=4), :]
                cls_scr[c, t2, r4] = cls
                emit(d4_ref, r4, slice(c * per4, (c + 1) * per4), cls)
            for r4 in range(4):
                for hi in range(4):
                    emit(d16_ref, r4 + 4 * hi, slice(c * per16, (c + 1) * per16),
                         cls_scr[c, t2, r4, pl.ds(hi, per16, stride=4), :])

        cqn = _rms(z[:, :Q_LORA_RANK], gq_ref[...])
        qa = _dot(cqn.astype(BF16), wuq_ref[...])
        scale = MLA_QK_DIM ** -0.5 * LOG2E
        for h in range(MLA_HEADS):
            hs = slice(h * LANES, (h + 1) * LANES)
            qkv_ref[rs, hs] = (_rotary(qa[:, hs], ct, st, x1_lanes) * scale).astype(BF16)

        ckvn = _rms(z[:, Q_LORA_RANK:C_KR], gkv_ref[...])
        kv = _dot(ckvn.astype(BF16), wukv_ref[...])
        qkv_ref[rs, 2 * QPAD:] = kv[:, QPAD:].astype(BF16)
        kpe = _rotary(z[:, C_KR:C_QD], ct, st, x1_lanes)
        for h in range(MLA_HEADS):
            hs = slice(h * LANES, (h + 1) * LANES)
            qkv_ref[rs, QPAD + h * LANES:QPAD + (h + 1) * LANES] = (kv[:, hs] + kpe).astype(BF16)


def _prep_mixer_weights(w_in, w_uq, w_ukv):
    depth = w_in.shape[0]
    w_uq, w_ukv = w_uq.astype(BF16), w_ukv.astype(BF16)
    o3 = Q_LORA_RANK + KV_LORA_RANK + MLA_ROPE_DIM
    wlat, wdil = w_in[..., :o3].astype(BF16), w_in[..., o3:].astype(BF16)

    wq = w_uq.reshape(depth, Q_LORA_RANK, MLA_HEADS, MLA_QK_DIM)
    zq = jnp.zeros((depth, Q_LORA_RANK, MLA_HEADS, 32), BF16)
    wuq = jnp.concatenate([wq, zq], axis=-1).reshape(depth, Q_LORA_RANK, QPAD)

    wkv = w_ukv.reshape(depth, KV_LORA_RANK, MLA_HEADS, MLA_NOPE_DIM + MLA_V_DIM)
    wk = jnp.concatenate([wkv[..., :MLA_NOPE_DIM], jnp.zeros_like(wkv[..., :MLA_NOPE_DIM])], axis=-1)
    wv = wkv[..., MLA_NOPE_DIM:]
    wukv = jnp.concatenate([wk.reshape(depth, KV_LORA_RANK, QPAD), wv.reshape(depth, KV_LORA_RANK, MLA_WIDTH)],
                           axis=-1)
    return wlat, wdil, wuq, wukv


def _mixer_in(l, x2d, seq, gpre, wlat, wdil, gq, wuq, gkv, wukv, ct, st, tm=512):
    n = x2d.shape[0]
    b = n // seq
    tpb = seq // tm
    row = lambda w: pl.BlockSpec((tm, w), lambda i: (i, 0))
    full = lambda a: _layer_spec(a, l)
    outs = [(2 * QPAD + MLA_WIDTH, BF16), (3 * DIL_WIDTH, BF16)]
    strided_specs = [pl.BlockSpec((1, dil, tm // dil, 3 * DIL_WIDTH), lambda i: (i // tpb, 0, i % tpb, 0))
                     for dil in STRIDED_DILATIONS]
    strided_shapes = [jax.ShapeDtypeStruct((b, dil, seq // dil, 3 * DIL_WIDTH), BF16) for dil in STRIDED_DILATIONS]
    return pl.pallas_call(
        _mixer_in_kernel,
        grid=(n // tm,),
        in_specs=[row(D_MODEL), full(gpre), full(wlat), full(wdil), full(gq), full(wuq), full(gkv), full(wukv),
                  row(LANES), row(LANES)],
        out_specs=[row(w) for w, _ in outs] + strided_specs,
        out_shape=[jax.ShapeDtypeStruct((n, w), dt) for w, dt in outs] + strided_shapes,
        scratch_shapes=[pltpu.VMEM((tm // ROW_SUBTILE, DIL_SLABS // 2, ROW_SUBTILE, LANES), jnp.uint32),
                        pltpu.VMEM((tm // ROW_SUBTILE, DIL_SLABS // 2, 4, ROW_SUBTILE // 4, LANES), jnp.uint32)],
        compiler_params=_params("parallel"),
        name="mixer_in",
    )(x2d, gpre, wlat, wdil, gq, wuq, gkv, wukv, ct, st)


MLA_HEADS_PER_STEP = 4


def _mla_attn_kernel(q_ref, k_ref, v_ref, o_ref, s_scr, m_scr, r_scr, acc_scr, oe_scr, *, tq, nq, nh):
    lo = _low_half()

    def rows(i):
        return slice(i * tq, (i + 1) * tq)

    def block(qi, j):
        return qi * (qi + 1) // 2 + j

    def pieces(qi, j):
        if j < qi:
            return [(0, tq, tq)]
        return [(0, tq // 2, tq // 2), (tq // 2, tq // 2, tq)]

    def scores(h, qi, j):
        slot = h % 2
        hs = slice(h * LANES, (h + 1) * LANES)
        for r0, nr, nk in pieces(qi, j):
            rs = slice(r0, r0 + nr)
            s = _dot_nt(q_ref[0, qi * tq + r0:qi * tq + r0 + nr, hs], k_ref[0, j * tq:j * tq + nk, hs])
            if j == qi:
                row = lax.broadcasted_iota(jnp.int32, (nr, nk), 0) + r0
                col = lax.broadcasted_iota(jnp.int32, (nr, nk), 1)
                s = jnp.where(col <= row, s, NEG_BIG)
            s_scr[slot, block(qi, j), rs, :nk] = s
            m = s[:, :LANES] if j == 0 else jnp.maximum(m_scr[slot, qi, rs], s[:, :LANES])
            for t in range(1, nk // LANES):
                m = jnp.maximum(m, s[:, t * LANES:(t + 1) * LANES])
            if j == qi:
                r_scr[slot, qi, rs] = jnp.broadcast_to(jnp.max(m, axis=-1, keepdims=True), (nr, LANES))
            else:
                m_scr[slot, qi, rs] = m

    def weigh(h, qi, j):
        slot = h % 2
        pr = slice((h // 2) * LANES, (h // 2 + 1) * LANES)
        for r0, nr, nk in pieces(qi, j):
            rs = slice(r0, r0 + nr)
            vblk = v_ref[0, j * tq:j * tq + nk, pr]
            s = s_scr[slot, block(qi, j), rs, :nk]
            rmax = r_scr[slot, qi, rs]
            p = jnp.concatenate([jnp.exp2(s[:, t * LANES:(t + 1) * LANES] - rmax) for t in range(nk // LANES)],
                                axis=1)
            pv = _dot(p.astype(BF16), jnp.concatenate([vblk, jnp.ones_like(vblk)], axis=1))
            if j == 0:
                acc_scr[slot, qi, rs] = pv
            else:
                acc_scr[slot, qi, rs] += pv
        if j == qi:
            acc = acc_scr[slot, qi]
            o = acc[:, :LANES] / acc[:, LANES:]
            if h % 2 == 0:
                oe_scr[qi] = o
            else:
                o_ref[0, rows(qi), pr] = jnp.where(lo, oe_scr[qi], o).astype(o_ref.dtype)

    for stage in range(nh + 1):
        for qi in range(nq):
            for j in range(qi + 1):
                if stage < nh:
                    scores(stage, qi, j)
                if stage > 0:
                    weigh(stage - 1, qi, j)


def _mla_attention(qkv, tq=512, nh=MLA_HEADS_PER_STEP):
    b, s, _ = qkv.shape
    nq = s // tq
    groups = MLA_HEADS // nh
    seq = lambda w, first: pl.BlockSpec((1, s, w), lambda bi, hg: (bi, 0, first + hg))
    return pl.pallas_call(
        functools.partial(_mla_attn_kernel, tq=tq, nq=nq, nh=nh),
        grid=(b, MLA_HEADS // nh),
        in_specs=[seq(nh * LANES, 0), seq(nh * LANES, groups), seq(nh * HALF, 2 * QPAD // (nh * HALF))],
        out_specs=seq(nh * HALF, 0),
        out_shape=jax.ShapeDtypeStruct((b, s, MLA_WIDTH), BF16),
        scratch_shapes=[pltpu.VMEM((2, nq * (nq + 1) // 2, tq, tq), F32),
                        pltpu.VMEM((2, nq, tq, LANES), F32),
                        pltpu.VMEM((2, nq, tq, LANES), F32),
                        pltpu.VMEM((2, nq, tq, 2 * LANES), F32),
                        pltpu.VMEM((nq, tq, LANES), F32)],
        compiler_params=_params("parallel", "parallel"),
        name="mla_attn",
    )(qkv, qkv, qkv)


def _dil_pair(q, kk, vv, bias2, lo):
    zero = jnp.zeros_like(q)
    q2 = jnp.concatenate([jnp.where(lo, q, zero), jnp.where(lo, zero, q)], axis=0)
    s = _dot_nt(q2, kk) + bias2
    m = jnp.max(s, axis=-1, keepdims=True)
    e = jnp.exp2(s - m).astype(BF16)
    r = _dot(e, jnp.concatenate([vv, jnp.ones_like(vv)], axis=1))
    mb = jnp.broadcast_to(m, (2 * DIL_N, LANES))
    pick = lambda a: jnp.where(lo, a[:DIL_N], a[DIL_N:])
    return pick(r[:, :LANES]), pick(r[:, LANES:]), pick(mb)


def _dil_pair_first(q, kk, vv, bias2, lo):
    zk, zv = jnp.zeros_like(kk), jnp.zeros_like(vv)
    ones_lo = jnp.broadcast_to(jnp.where(lo, 1.0, 0.0), vv.shape).astype(BF16)
    ones_hi = jnp.broadcast_to(jnp.where(lo, 0.0, 1.0), vv.shape).astype(BF16)
    keys2 = jnp.concatenate([jnp.where(lo, kk, zk), jnp.where(lo, zk, kk)], axis=0)
    s = _dot_nt(q, keys2) + jnp.concatenate([bias2[:DIL_N, :DIL_N], bias2[DIL_N:, :DIL_N]], axis=1)
    m0 = jnp.max(s[:, :DIL_N], axis=-1, keepdims=True)
    m1 = jnp.max(s[:, DIL_N:], axis=-1, keepdims=True)
    e = jnp.concatenate([jnp.exp2(s[:, :DIL_N] - m0), jnp.exp2(s[:, DIL_N:] - m1)], axis=1).astype(BF16)
    w = jnp.concatenate([jnp.concatenate([jnp.where(lo, vv, zv), ones_lo], axis=1),
                         jnp.concatenate([jnp.where(lo, zv, vv), ones_hi], axis=1)], axis=0)
    r = _dot(e, w)
    return r[:, :LANES], r[:, LANES:], jnp.where(lo, m0, m1)


def _dilated_kernel(q_ref, k_ref, v_ref, q4_ref, k4_ref, v4_ref, q16_ref, k16_ref, v16_ref, bias_ref, o_ref,
                    *acc_scr, seq):
    lo = _low_half()
    acc = [acc_scr[3 * p:3 * p + 3] for p in range(len(DIL_PATTERNS))]
    strided = {4: (q4_ref, k4_ref, v4_ref), 16: (q16_ref, k16_ref, v16_ref)}

    def chunk(p, dil, u, first):
        r, c = (u, 0) if first else (u % dil, 1 + u // dil)
        row_q = c * DIL_N
        row_k, nk = (row_q, DIL_N) if first else (row_q - DIL_N, 2 * DIL_N)
        if dil == 1:
            q = q_ref[0, pl.ds(row_q, DIL_N), :]
            kk = k_ref[0, pl.ds(row_k, nk), :]
            vv = v_ref[0, pl.ds(row_k, nk), :]
        else:
            qr, kr, vr = strided[dil]
            q = qr[0, r, pl.ds(row_q, DIL_N), :]
            kk = kr[0, r, pl.ds(row_k, nk), :]
            vv = vr[0, r, pl.ds(row_k, nk), :]
        if first:
            res = _dil_pair_first(q, kk, vv, bias_ref[p, 0, 0], lo)
        else:
            res = _dil_pair(q, kk, vv, bias_ref[p, 1, 0], lo)
        t0 = r + dil * DIL_N * c
        out_rows = pl.ds(t0, DIL_N) if dil == 1 else pl.ds(t0, DIL_N, stride=dil)
        return out_rows, res

    def chunks(p, dil, us, first):
        results = [chunk(p, dil, u, first) for u in us]
        for out_rows, res in results:
            for dst, val in zip(acc[p], res):
                dst[out_rows, :] = val

    def merge(rs):
        ms = [a[2][rs, :] for a in acc]
        mx = functools.reduce(jnp.maximum, ms)
        fs = [jnp.exp2(m - mx) for m in ms]
        num = sum(f * a[0][rs, :] for f, a in zip(fs, acc))
        den = sum(f * a[1][rs, :] for f, a in zip(fs, acc))
        o_ref[0, rs, :] = (num / den).astype(o_ref.dtype)

    for p, (window, dil) in enumerate(DIL_PATTERNS):
        if dil > 1:
            chunks(p, dil, range(dil), True)
            if seq > window:
                chunks(p, dil, range(dil * (seq // window - 1)), False)
    (p1, (window1, _)), = [(p, pat) for p, pat in enumerate(DIL_PATTERNS) if pat[1] == 1]
    assert window1 == DIL_N
    group = 2
    for g0 in range(0, seq // DIL_N, group):
        if g0 == 0:
            chunks(p1, 1, [0], True)
            chunks(p1, 1, list(range(group - 1)), False)
        else:
            chunks(p1, 1, [c - 1 for c in range(g0, g0 + group)], False)
        merge(slice(g0 * DIL_N, (g0 + group) * DIL_N))


def _dilated_attention(nat, strided, bias):
    b, s, _ = nat.shape
    w = DIL_WIDTH
    npat = len(DIL_PATTERNS)
    tiles = w // LANES
    blk = pl.BlockSpec((1, s, LANES), lambda bi, hp: (bi, 0, hp))
    nat_specs = [pl.BlockSpec((1, s, LANES), lambda bi, hp, sec=sec: (bi, 0, sec * tiles + hp)) for sec in range(3)]
    strided_specs, strided_args = [], []
    for arr, dil in zip(strided, STRIDED_DILATIONS):
        for sec in range(3):
            strided_specs.append(pl.BlockSpec((1, dil, s // dil, LANES),
                                              lambda bi, hp, sec=sec: (bi, 0, 0, sec * tiles + hp)))
            strided_args.append(arr)
    return pl.pallas_call(
        functools.partial(_dilated_kernel, seq=s),
        grid=(b, DIL_HEADS // 2),
        in_specs=nat_specs + strided_specs
                 + [pl.BlockSpec((npat, 2, 1, 2 * DIL_N, 2 * DIL_N), lambda bi, hp: (0, 0, hp, 0, 0))],
        out_specs=blk,
        out_shape=jax.ShapeDtypeStruct((b, s, w), BF16),
        scratch_shapes=[pltpu.VMEM((s, LANES), F32)] * (3 * npat),
        compiler_params=_params("parallel", "parallel"),
        name="dilated",
    )(nat, nat, nat, *strided_args, bias)


def _mem_kv_kernel(mem_ref, g_ref, wk_ref, wv_ref, k_ref, v_ref):
    mn = _rms(mem_ref[0], g_ref[...]).astype(BF16)
    k_ref[0] = _dot(mn, wk_ref[...].astype(BF16)).astype(BF16)
    v_ref[0] = _dot(mn, wv_ref[...].astype(BF16)).astype(BF16)


def _mem_kv(mem, g, wk, wv):
    b, m, d = mem.shape
    depth = wk.shape[0]
    layer = lambda a: pl.BlockSpec((None,) + a.shape[1:], lambda l, i: (l,) + (0,) * (a.ndim - 1),
                                   pipeline_mode=pl.Buffered(1))
    out = pl.BlockSpec((None, 1, m, d), lambda l, i: (l, i, 0, 0))
    return pl.pallas_call(
        _mem_kv_kernel,
        grid=(depth, b),
        in_specs=[pl.BlockSpec((1, m, d), lambda l, i: (i, 0, 0)), layer(g), layer(wk), layer(wv)],
        out_specs=[out, out],
        out_shape=[jax.ShapeDtypeStruct((depth, b, m, d), BF16)] * 2,
        compiler_params=_params("arbitrary", "parallel"),
        name="mem_kv",
    )(mem, g, wk, wv)


def _post_mixer_kernel(x_ref, om_ref, od_ref, wout_ref, gpm_ref, gpx_ref, wxq_ref, kx_ref, vx_ref,
                       wxo_ref, gpox_ref, wup_ref, wdn_ref, out_ref, wup_bf_ref, wdn_bf_ref):
    wup_bf_ref[...] = wup_ref[...].astype(BF16)
    wdn_bf_ref[...] = wdn_ref[...].astype(BF16)
    subs = [slice(c * ROW_SUBTILE, (c + 1) * ROW_SUBTILE) for c in range(x_ref.shape[1] // ROW_SUBTILE)]
    wout_a, wout_b = wout_ref[:MLA_WIDTH, :].astype(BF16), wout_ref[MLA_WIDTH:, :].astype(BF16)
    wxq, wxo = wxq_ref[...].astype(BF16), wxo_ref[...].astype(BF16)
    a = [_dot(om_ref[0, rs], wout_a) + _dot(od_ref[0, rs], wout_b) for rs in subs]
    x1 = [x_ref[0, rs] + _rms(ai, gpm_ref[...]) for rs, ai in zip(subs, a)]
    q = [(_dot(_rms(xi, gpx_ref[...]).astype(BF16), wxq) * (X_HEAD_DIM ** -0.5 * LOG2E)).astype(BF16)
         for xi in x1]
    o = []
    for qi in q:
        heads = []
        for h in range(X_HEADS):
            sl = slice(h * X_HEAD_DIM, (h + 1) * X_HEAD_DIM)
            s = _dot_nt(qi[:, sl], kx_ref[0, :, sl])
            m = jnp.max(s, axis=-1, keepdims=True)
            e = jnp.exp2(s - m)
            den = jnp.sum(e, axis=-1, keepdims=True)
            heads.append((_dot(e.astype(BF16), vx_ref[0, :, sl]) * (1.0 / den)).astype(BF16))
        o.append(jnp.concatenate(heads, axis=-1))
    cx = [_dot(oi, wxo) for oi in o]
    for rs, xi, ci in zip(subs, x1, cx):
        out_ref[0, rs] = xi + _rms(ci, gpox_ref[...])


def _post_mixer(l, x, om, od, wout, gpm, gpx, wxq, kx, vx, wxo, gpox, wup, wdn, tm=1024):
    b, s, d = x.shape
    steps = b * (s // tm)
    tok = lambda w: pl.BlockSpec((1, tm, w), lambda bi, i: (bi, i, 0))
    full = lambda a: _layer_spec(a, l)
    memb = pl.BlockSpec((None, 1, MEM_LEN, d), lambda bi, i: (l, bi, 0, 0))
    band_in = lambda a: pl.BlockSpec((None, a.shape[1] // steps, a.shape[2]),
                                     lambda bi, i: (l, bi * (s // tm) + i, 0))
    band_out = lambda a: pl.BlockSpec((a.shape[1] // steps, a.shape[2]), lambda bi, i: (bi * (s // tm) + i, 0))
    return pl.pallas_call(
        _post_mixer_kernel,
        grid=(b, s // tm),
        in_specs=[tok(d), tok(MLA_WIDTH), tok(DIL_WIDTH),
                  full(wout), full(gpm), full(gpx), full(wxq), memb, memb, full(wxo), full(gpox),
                  band_in(wup), band_in(wdn)],
        out_specs=[tok(d), band_out(wup), band_out(wdn)],
        out_shape=[jax.ShapeDtypeStruct((b, s, d), F32),
                   jax.ShapeDtypeStruct(wup.shape[1:], BF16), jax.ShapeDtypeStruct(wdn.shape[1:], BF16)],
        compiler_params=_params("parallel", "parallel"),
        name="post_mixer",
    )(x, om, od, wout, gpm, gpx, wxq, kx, vx, wxo, gpox, wup, wdn)


def _mlp_kernel(x_ref, gpre_ref, wup_ref, wdn_ref, gpost_ref, out_ref):
    subs = [slice(c * ROW_SUBTILE, (c + 1) * ROW_SUBTILE) for c in range(x_ref.shape[0] // ROW_SUBTILE)]

    def up(rs):
        u = jnp.maximum(_dot(_rms(x_ref[rs, :], gpre_ref[...]).astype(BF16), wup_ref[...]), 0.0)
        return (u * u).astype(BF16)

    def down(rs, h):
        out_ref[rs, :] = x_ref[rs, :] + _rms(_dot(h, wdn_ref[...]), gpost_ref[...])

    h = up(subs[0])
    for c, rs in enumerate(subs):
        h_next = up(subs[c + 1]) if c + 1 < len(subs) else None
        down(rs, h)
        h = h_next


def _mlp(l, x2d, gpre, wup, wdn, gpost, tm=1024):
    n, d = x2d.shape
    row = pl.BlockSpec((tm, d), lambda i: (i, 0))
    full = lambda a: _layer_spec(a, l)
    whole = lambda a: pl.BlockSpec(a.shape, lambda i: (0, 0), pipeline_mode=pl.Buffered(1))
    return pl.pallas_call(
        _mlp_kernel,
        grid=(n // tm,),
        in_specs=[row, full(gpre), whole(wup), whole(wdn), full(gpost)],
        out_specs=row,
        out_shape=jax.ShapeDtypeStruct((n, d), F32),
        compiler_params=_params("parallel"),
        name="mlp",
    )(x2d, gpre, wup, wdn, gpost)


def kernel(x, mem, positions, rel_bias, w_in, q_norm_g, w_uq, kv_norm_g, w_ukv, w_mix_out, pre_mix_g, post_mix_g, w_xq, w_xk, w_xv, w_xo, mem_g, pre_xattn_g, post_xattn_g, w_up, w_down, pre_mlp_g, post_mlp_g):
    b, s, d = x.shape
    n = b * s
    ct, st = _rope_tables(positions)
    bias = _bias_tables(rel_bias).reshape(len(DIL_PATTERNS), 2, DIL_HEADS // 2, 2 * DIL_N, 2 * DIL_N)
    gain = lambda a: a.reshape(a.shape[0], 1, -1).astype(F32)
    gains = {k: gain(v) for k, v in dict(
        pre_mix=pre_mix_g, q_norm=q_norm_g, kv_norm=kv_norm_g, post_mix=post_mix_g, mem=mem_g,
        pre_x=pre_xattn_g, post_x=post_xattn_g, pre_mlp=pre_mlp_g, post_mlp=post_mlp_g).items()}
    wlat, wdil, wuq, wukv = _prep_mixer_weights(w_in, w_uq, w_ukv)
    kx, vx = _mem_kv(mem, gains["mem"], w_xk, w_xv)
    for l in range(DEPTH):
        qkv, nat, *strided = _mixer_in(l, x.reshape(n, d), s, gains["pre_mix"], wlat, wdil, gains["q_norm"], wuq,
                                       gains["kv_norm"], wukv, ct, st)
        om = _mla_attention(qkv.reshape(b, s, -1))
        od = _dilated_attention(nat.reshape(b, s, -1), strided, bias)
        x, wup, wdn = _post_mixer(l, x, om, od, w_mix_out, gains["post_mix"], gains["pre_x"], w_xq, kx, vx, w_xo,
                                  gains["post_x"], w_up, w_down)
        x = _mlp(l, x.reshape(n, d), gains["pre_mlp"], wup, wdn, gains["post_mlp"]).reshape(b, s, d)
    return x
```

```python
import functools
import math

import numpy as np
import jax
import jax.numpy as jnp
from jax import lax
from jax.experimental import pallas as pl
from jax.experimental.pallas import tpu as pltpu

D_MODEL = 1024
DEPTH = 2
MEM_LEN = 256
MLA_HEADS = 8
MLA_NOPE_DIM = 64
MLA_ROPE_DIM = 32
MLA_V_DIM = 64
Q_LORA_RANK = 384
KV_LORA_RANK = 256
ROPE_THETA = 10000.0
DIL_HEADS = 8
DIL_HEAD_DIM = 64
DIL_PATTERNS = ((128, 1), (512, 4), (2048, 16))
N_BUCKETS = 32
MAX_DISTANCE = 2048
X_HEADS = 4
X_HEAD_DIM = D_MODEL // X_HEADS
D_FF = 4 * D_MODEL
EPS = 1e-6

MLA_QK_DIM = MLA_NOPE_DIM + MLA_ROPE_DIM
DIL_WIDTH = DIL_HEADS * DIL_HEAD_DIM
MLA_WIDTH = MLA_HEADS * MLA_V_DIM
ROPE_HALF = MLA_ROPE_DIM // 2
DIL_N = 128
LANES = 128
HALF = LANES // 2
ROW_SUBTILE = 256
NEG_BIG = -1e30
LOG2E = math.log2(math.e)
VMEM_LIMIT_BYTES = 56 * 1024 * 1024

BF16 = jnp.bfloat16
F32 = jnp.float32


def _rms(x, g):
    return x * lax.rsqrt(jnp.mean(x * x, axis=-1, keepdims=True) + EPS) * g


def _dot(a, b):
    return jnp.dot(a, b, preferred_element_type=F32)


def _dot_nt(a, b):
    return lax.dot_general(a, b, (((1,), (1,)), ((), ())), preferred_element_type=F32)


def _params(*sem):
    return pltpu.CompilerParams(dimension_semantics=sem, vmem_limit_bytes=VMEM_LIMIT_BYTES)


def _layer_spec(a, l):
    return pl.BlockSpec((None,) + a.shape[1:], lambda *_: (l,) + (0,) * (a.ndim - 1),
                        pipeline_mode=pl.Buffered(1))


def _low_half():
    return lax.broadcasted_iota(jnp.int32, (1, LANES), 1) < HALF


TOKENS_PER_ROW = LANES // ROPE_HALF


def _rope_table_kernel(pos_ref, inv_ref, ct_ref, st_ref):
    tr = pos_ref.shape[0]
    ang = pos_ref[...] * inv_ref[...]
    cos = jnp.cos(ang)
    sin = jnp.sin(ang)
    lane = lax.broadcasted_iota(jnp.int32, (1, LANES), 1)
    x1 = jnp.logical_and(lane >= MLA_NOPE_DIM, lane < MLA_NOPE_DIM + ROPE_HALF)
    x2 = jnp.logical_and(lane >= MLA_NOPE_DIM + ROPE_HALF, lane < MLA_QK_DIM)
    for j in range(TOKENS_PER_ROW):
        to_x1 = (MLA_NOPE_DIM - ROPE_HALF * j) % LANES
        to_x2 = (MLA_NOPE_DIM + ROPE_HALF - ROPE_HALF * j) % LANES
        c1, c2 = pltpu.roll(cos, to_x1, axis=1), pltpu.roll(cos, to_x2, axis=1)
        s1, s2 = pltpu.roll(sin, to_x1, axis=1), pltpu.roll(sin, to_x2, axis=1)
        rows = pl.ds(j, tr, stride=TOKENS_PER_ROW)
        ct_ref[rows, :] = jnp.where(lane < MLA_NOPE_DIM, 1.0, jnp.where(x1, c1, jnp.where(x2, c2, 0.0)))
        st_ref[rows, :] = jnp.where(x1, -s1, jnp.where(x2, s2, 0.0))


def _rope_tables(positions):
    n = positions.size
    rows = n // TOKENS_PER_ROW
    pos_rep = jnp.repeat(positions.reshape(rows, TOKENS_PER_ROW).astype(F32), ROPE_HALF, axis=1)
    inv = ROPE_THETA ** (-jnp.arange(ROPE_HALF, dtype=F32) / ROPE_HALF)
    inv_rep = jnp.tile(inv, TOKENS_PER_ROW).reshape(1, LANES)
    tr = 256
    return pl.pallas_call(
        _rope_table_kernel,
        grid=(rows // tr,),
        in_specs=[pl.BlockSpec((tr, LANES), lambda i: (i, 0)),
                  pl.BlockSpec((1, LANES), lambda i: (0, 0))],
        out_specs=[pl.BlockSpec((tr * TOKENS_PER_ROW, LANES), lambda i: (i, 0))] * 2,
        out_shape=[jax.ShapeDtypeStruct((n, LANES), F32)] * 2,
        compiler_params=_params("parallel"),
        name="rope_tables",
    )(pos_rep, inv_rep)


def _bucket_tables():
    i = np.arange(DIL_N)[:, None]
    m = np.arange(2 * DIL_N)[None, :]
    rel = DIL_N + i - m
    band = (rel >= 0) & (rel <= DIL_N)
    max_exact = N_BUCKETS // 2
    out = []
    for _, dil in DIL_PATTERNS:
        d = np.maximum(rel * dil, 0)
        df = np.maximum(d.astype(np.float64), 1.0)
        large = max_exact + (np.log(df / max_exact) / math.log(MAX_DISTANCE / max_exact)
                             * (N_BUCKETS - max_exact)).astype(np.int32)
        large = np.minimum(large, N_BUCKETS - 1)
        bucket = np.where(d < max_exact, d, large)
        out.append(np.where(band, bucket, -1).astype(np.int32))
    return np.stack(out)


def _bias_table_kernel(rb_ref, bucket_ref, out_ref):
    bucket = bucket_ref[0]
    masked = jnp.full((DIL_N, DIL_N), NEG_BIG, F32)
    for h in range(DIL_HEADS):
        tab = jnp.full(bucket.shape, NEG_BIG, F32)
        for b in range(N_BUCKETS):
            tab = jnp.where(bucket == b, rb_ref[b, h] * LOG2E, tab)
        out_ref[0, 0, h] = jnp.concatenate([tab[:, DIL_N:], masked], axis=1)
        out_ref[0, 1, h] = tab


def _bias_tables(rel_bias):
    buckets = jnp.asarray(_bucket_tables())
    npat = len(DIL_PATTERNS)
    return pl.pallas_call(
        _bias_table_kernel,
        grid=(npat,),
        in_specs=[pl.BlockSpec(memory_space=pltpu.SMEM),
                  pl.BlockSpec((1, DIL_N, 2 * DIL_N), lambda p: (p, 0, 0))],
        out_specs=pl.BlockSpec((1, 2, DIL_HEADS, DIL_N, 2 * DIL_N), lambda p: (p, 0, 0, 0, 0)),
        out_shape=jax.ShapeDtypeStruct((npat, 2, DIL_HEADS, DIL_N, 2 * DIL_N), F32),
        compiler_params=_params("parallel"),
        name="bias_tables",
    )(rel_bias.astype(F32), buckets)


C_KR = Q_LORA_RANK + KV_LORA_RANK
C_QD = C_KR + LANES
QPAD = MLA_HEADS * LANES
DIL_SLABS = 3 * DIL_WIDTH // LANES
STRIDED_DILATIONS = tuple(d for _, d in DIL_PATTERNS if d > 1)


def _rotary(a, ct, st, x1_lanes):
    partner = jnp.where(x1_lanes, pltpu.roll(a, LANES - ROPE_HALF, axis=1), pltpu.roll(a, ROPE_HALF, axis=1))
    return a * ct + partner * st


def _mixer_in_kernel(x_ref, gpre_ref, wlat_ref, wdil_ref, gq_ref, wuq_ref, gkv_ref, wukv_ref, ct_ref, st_ref,
                     mem_ref, gmem_ref, wxk_ref, wxv_ref,
                     qkv_ref, nat_ref, d4_ref, d16_ref, kx_ref, vx_ref, slab_scr, cls_scr, *, tiles_per_batch):
    @pl.when(pl.program_id(0) % tiles_per_batch == 0)
    def _():
        mn = _rms(mem_ref[0], gmem_ref[...]).astype(BF16)
        kx_ref[0] = _dot(mn, wxk_ref[...].astype(BF16)).astype(BF16)
        vx_ref[0] = _dot(mn, wxv_ref[...].astype(BF16)).astype(BF16)

    lane = lax.broadcasted_iota(jnp.int32, (1, LANES), 1)
    x1_lanes = jnp.logical_and(lane >= MLA_NOPE_DIM, lane < MLA_NOPE_DIM + ROPE_HALF)
    nsec = DIL_SLABS // 3

    subs = [slice(c * ROW_SUBTILE, (c + 1) * ROW_SUBTILE) for c in range(x_ref.shape[0] // ROW_SUBTILE)]
    wl = wlat_ref[...]
    zeros = lambda n: jnp.zeros((wl.shape[0], n), BF16)
    w_lat = jnp.concatenate([wl[:, :C_KR], zeros(MLA_NOPE_DIM), wl[:, C_KR:], zeros(LANES - MLA_QK_DIM)], axis=1)
    xn = [_rms(x_ref[rs, :], gpre_ref[...]).astype(BF16) for rs in subs]
    zls = [_dot(xi, w_lat) for xi in xn]
    zds = [_dot(xi, wdil_ref[...]) for xi in xn]
    for c, (rs, z, zd) in enumerate(zip(subs, zls, zds)):
        ct = ct_ref[rs, :]
        st = st_ref[rs, :]

        per4, per16 = ROW_SUBTILE // 4, ROW_SUBTILE // 16
        for t2 in range(DIL_SLABS // 2):
            vals = []
            for t in (2 * t2, 2 * t2 + 1):
                val = zd[:, t * LANES:(t + 1) * LANES]
                if t < nsec:
                    val = val * (DIL_HEAD_DIM ** -0.5 * LOG2E)
                nat_ref[rs, t * LANES:(t + 1) * LANES] = val.astype(BF16)
                vals.append(val)
            slab_scr[c, t2] = pltpu.pack_elementwise(vals, packed_dtype=BF16)

            def emit(dref, r, rows, packed):
                for i in range(2):
                    tile = pltpu.unpack_elementwise(packed, index=i, packed_dtype=BF16, unpacked_dtype=F32)
                    dref[0, r, rows, (2 * t2 + i) * LANES:(2 * t2 + i + 1) * LANES] = tile.astype(BF16)

            for r4 in range(4):
                cls = slab_scr[c, t2, pl.ds(r4, per4, stride=4), :]
                cls_scr[c, t2, r4] = cls
                emit(d4_ref, r4, slice(c * per4, (c + 1) * per4), cls)
            for r4 in range(4):
                for hi in range(4):
                    emit(d16_ref, r4 + 4 * hi, slice(c * per16, (c + 1) * per16),
                         cls_scr[c, t2, r4, pl.ds(hi, per16, stride=4), :])

        cqn = _rms(z[:, :Q_LORA_RANK], gq_ref[...])
        qa = _dot(cqn.astype(BF16), wuq_ref[...])
        scale = MLA_QK_DIM ** -0.5 * LOG2E
        for h in range(MLA_HEADS):
            hs = slice(h * LANES, (h + 1) * LANES)
            qkv_ref[rs, hs] = (_rotary(qa[:, hs], ct, st, x1_lanes) * scale).astype(BF16)

        ckvn = _rms(z[:, Q_LORA_RANK:C_KR], gkv_ref[...])
        kv = _dot(ckvn.astype(BF16), wukv_ref[...])
        qkv_ref[rs, 2 * QPAD:] = kv[:, QPAD:].astype(BF16)
        kpe = _rotary(z[:, C_KR:C_QD], ct, st, x1_lanes)
        for h in range(MLA_HEADS):
            hs = slice(h * LANES, (h + 1) * LANES)
            qkv_ref[rs, QPAD + h * LANES:QPAD + (h + 1) * LANES] = (kv[:, hs] + kpe).astype(BF16)


def _prep_mixer_weights(w_in, w_uq, w_ukv):
    depth = w_in.shape[0]
    w_uq, w_ukv = w_uq.astype(BF16), w_ukv.astype(BF16)
    o3 = Q_LORA_RANK + KV_LORA_RANK + MLA_ROPE_DIM
    wlat, wdil = w_in[..., :o3].astype(BF16), w_in[..., o3:].astype(BF16)

    wq = w_uq.reshape(depth, Q_LORA_RANK, MLA_HEADS, MLA_QK_DIM)
    zq = jnp.zeros((depth, Q_LORA_RANK, MLA_HEADS, 32), BF16)
    wuq = jnp.concatenate([wq, zq], axis=-1).reshape(depth, Q_LORA_RANK, QPAD)

    wkv = w_ukv.reshape(depth, KV_LORA_RANK, MLA_HEADS, MLA_NOPE_DIM + MLA_V_DIM)
    wk = jnp.concatenate([wkv[..., :MLA_NOPE_DIM], jnp.zeros_like(wkv[..., :MLA_NOPE_DIM])], axis=-1)
    wv = wkv[..., MLA_NOPE_DIM:]
    wukv = jnp.concatenate([wk.reshape(depth, KV_LORA_RANK, QPAD), wv.reshape(depth, KV_LORA_RANK, MLA_WIDTH)],
                           axis=-1)
    return wlat, wdil, wuq, wukv


def _mixer_in(l, x2d, seq, gpre, wlat, wdil, gq, wuq, gkv, wukv, ct, st, mem, gmem, wxk, wxv, tm=512):
    n = x2d.shape[0]
    b = n // seq
    tpb = seq // tm
    row = lambda w: pl.BlockSpec((tm, w), lambda i: (i, 0))
    full = lambda a: _layer_spec(a, l)
    outs = [(2 * QPAD + MLA_WIDTH, BF16), (3 * DIL_WIDTH, BF16)]
    memb = pl.BlockSpec((1,) + mem.shape[1:], lambda i: (i // tpb, 0, 0))
    strided_specs = [pl.BlockSpec((1, dil, tm // dil, 3 * DIL_WIDTH), lambda i: (i // tpb, 0, i % tpb, 0))
                     for dil in STRIDED_DILATIONS]
    strided_shapes = [jax.ShapeDtypeStruct((b, dil, seq // dil, 3 * DIL_WIDTH), BF16) for dil in STRIDED_DILATIONS]
    return pl.pallas_call(
        functools.partial(_mixer_in_kernel, tiles_per_batch=tpb),
        grid=(n // tm,),
        in_specs=[row(D_MODEL), full(gpre), full(wlat), full(wdil), full(gq), full(wuq), full(gkv), full(wukv),
                  row(LANES), row(LANES), memb, full(gmem), full(wxk), full(wxv)],
        out_specs=[row(w) for w, _ in outs] + strided_specs + [memb, memb],
        out_shape=([jax.ShapeDtypeStruct((n, w), dt) for w, dt in outs] + strided_shapes
                   + [jax.ShapeDtypeStruct(mem.shape, BF16)] * 2),
        scratch_shapes=[pltpu.VMEM((tm // ROW_SUBTILE, DIL_SLABS // 2, ROW_SUBTILE, LANES), jnp.uint32),
                        pltpu.VMEM((tm // ROW_SUBTILE, DIL_SLABS // 2, 4, ROW_SUBTILE // 4, LANES), jnp.uint32)],
        compiler_params=_params("arbitrary"),
        name="mixer_in",
    )(x2d, gpre, wlat, wdil, gq, wuq, gkv, wukv, ct, st, mem, gmem, wxk, wxv)


MLA_HEADS_PER_STEP = 4


def _mla_attn_kernel(q_ref, k_ref, v_ref, o_ref, s_scr, m_scr, r_scr, acc_scr, oe_scr, *, tq, nq, nh):
    lo = _low_half()

    def rows(i):
        return slice(i * tq, (i + 1) * tq)

    def block(qi, j):
        return qi * (qi + 1) // 2 + j

    def pieces(qi, j):
        if j < qi:
            return [(0, tq, tq)]
        return [(0, tq // 2, tq // 2), (tq // 2, tq // 2, tq)]

    def scores(h, qi, j):
        slot = h % 2
        hs = slice(h * LANES, (h + 1) * LANES)
        for r0, nr, nk in pieces(qi, j):
            rs = slice(r0, r0 + nr)
            s = _dot_nt(q_ref[0, qi * tq + r0:qi * tq + r0 + nr, hs], k_ref[0, j * tq:j * tq + nk, hs])
            if j == qi:
                row = lax.broadcasted_iota(jnp.int32, (nr, nk), 0) + r0
                col = lax.broadcasted_iota(jnp.int32, (nr, nk), 1)
                s = jnp.where(col <= row, s, NEG_BIG)
            s_scr[slot, block(qi, j), rs, :nk] = s
            m = s[:, :LANES] if j == 0 else jnp.maximum(m_scr[slot, qi, rs], s[:, :LANES])
            for t in range(1, nk // LANES):
                m = jnp.maximum(m, s[:, t * LANES:(t + 1) * LANES])
            if j == qi:
                r_scr[slot, qi, rs] = jnp.broadcast_to(jnp.max(m, axis=-1, keepdims=True), (nr, LANES))
            else:
                m_scr[slot, qi, rs] = m

    def weigh(h, qi, j):
        slot = h % 2
        pr = slice((h // 2) * LANES, (h // 2 + 1) * LANES)
        for r0, nr, nk in pieces(qi, j):
            rs = slice(r0, r0 + nr)
            vblk = v_ref[0, j * tq:j * tq + nk, pr]
            s = s_scr[slot, block(qi, j), rs, :nk]
            rmax = r_scr[slot, qi, rs]
            p = jnp.concatenate([jnp.exp2(s[:, t * LANES:(t + 1) * LANES] - rmax) for t in range(nk // LANES)],
                                axis=1)
            pv = _dot(p.astype(BF16), jnp.concatenate([vblk, jnp.ones_like(vblk)], axis=1))
            if j == 0:
                acc_scr[slot, qi, rs] = pv
            else:
                acc_scr[slot, qi, rs] += pv
        if j == qi:
            acc = acc_scr[slot, qi]
            o = acc[:, :LANES] / acc[:, LANES:]
            if h % 2 == 0:
                oe_scr[qi] = o
            else:
                o_ref[0, rows(qi), pr] = jnp.where(lo, oe_scr[qi], o).astype(o_ref.dtype)

    for stage in range(nh + 1):
        for qi in range(nq):
            for j in range(qi + 1):
                if stage < nh:
                    scores(stage, qi, j)
                if stage > 0:
                    weigh(stage - 1, qi, j)


def _mla_attention(qkv, tq=512, nh=MLA_HEADS_PER_STEP):
    b, s, _ = qkv.shape
    nq = s // tq
    groups = MLA_HEADS // nh
    seq = lambda w, first: pl.BlockSpec((1, s, w), lambda bi, hg: (bi, 0, first + hg))
    return pl.pallas_call(
        functools.partial(_mla_attn_kernel, tq=tq, nq=nq, nh=nh),
        grid=(b, MLA_HEADS // nh),
        in_specs=[seq(nh * LANES, 0), seq(nh * LANES, groups), seq(nh * HALF, 2 * QPAD // (nh * HALF))],
        out_specs=seq(nh * HALF, 0),
        out_shape=jax.ShapeDtypeStruct((b, s, MLA_WIDTH), BF16),
        scratch_shapes=[pltpu.VMEM((2, nq * (nq + 1) // 2, tq, tq), F32),
                        pltpu.VMEM((2, nq, tq, LANES), F32),
                        pltpu.VMEM((2, nq, tq, LANES), F32),
                        pltpu.VMEM((2, nq, tq, 2 * LANES), F32),
                        pltpu.VMEM((nq, tq, LANES), F32)],
        compiler_params=_params("parallel", "parallel"),
        name="mla_attn",
    )(qkv, qkv, qkv)


def _dil_pair(q, kk, vv, bias2, lo):
    zero = jnp.zeros_like(q)
    q2 = jnp.concatenate([jnp.where(lo, q, zero), jnp.where(lo, zero, q)], axis=0)
    s = _dot_nt(q2, kk) + bias2
    m = jnp.max(s, axis=-1, keepdims=True)
    e = jnp.exp2(s - m).astype(BF16)
    r = _dot(e, jnp.concatenate([vv, jnp.ones_like(vv)], axis=1))
    mb = jnp.broadcast_to(m, (2 * DIL_N, LANES))
    pick = lambda a: jnp.where(lo, a[:DIL_N], a[DIL_N:])
    return pick(r[:, :LANES]), pick(r[:, LANES:]), pick(mb)


def _dil_pair_first(q, kk, vv, bias2, lo):
    zk, zv = jnp.zeros_like(kk), jnp.zeros_like(vv)
    ones_lo = jnp.broadcast_to(jnp.where(lo, 1.0, 0.0), vv.shape).astype(BF16)
    ones_hi = jnp.broadcast_to(jnp.where(lo, 0.0, 1.0), vv.shape).astype(BF16)
    keys2 = jnp.concatenate([jnp.where(lo, kk, zk), jnp.where(lo, zk, kk)], axis=0)
    s = _dot_nt(q, keys2) + jnp.concatenate([bias2[:DIL_N, :DIL_N], bias2[DIL_N:, :DIL_N]], axis=1)
    m0 = jnp.max(s[:, :DIL_N], axis=-1, keepdims=True)
    m1 = jnp.max(s[:, DIL_N:], axis=-1, keepdims=True)
    e = jnp.concatenate([jnp.exp2(s[:, :DIL_N] - m0), jnp.exp2(s[:, DIL_N:] - m1)], axis=1).astype(BF16)
    w = jnp.concatenate([jnp.concatenate([jnp.where(lo, vv, zv), ones_lo], axis=1),
                         jnp.concatenate([jnp.where(lo, zv, vv), ones_hi], axis=1)], axis=0)
    r = _dot(e, w)
    return r[:, :LANES], r[:, LANES:], jnp.where(lo, m0, m1)


def _dilated_kernel(q_ref, k_ref, v_ref, q4_ref, k4_ref, v4_ref, q16_ref, k16_ref, v16_ref, bias_ref, o_ref,
                    *acc_scr, seq):
    lo = _low_half()
    acc = [acc_scr[3 * p:3 * p + 3] for p in range(len(DIL_PATTERNS))]
    strided = {4: (q4_ref, k4_ref, v4_ref), 16: (q16_ref, k16_ref, v16_ref)}

    def chunk(p, dil, u, first):
        r, c = (u, 0) if first else (u % dil, 1 + u // dil)
        row_q = c * DIL_N
        row_k, nk = (row_q, DIL_N) if first else (row_q - DIL_N, 2 * DIL_N)
        if dil == 1:
            q = q_ref[0, pl.ds(row_q, DIL_N), :]
            kk = k_ref[0, pl.ds(row_k, nk), :]
            vv = v_ref[0, pl.ds(row_k, nk), :]
        else:
            qr, kr, vr = strided[dil]
            q = qr[0, r, pl.ds(row_q, DIL_N), :]
            kk = kr[0, r, pl.ds(row_k, nk), :]
            vv = vr[0, r, pl.ds(row_k, nk), :]
        if first:
            res = _dil_pair_first(q, kk, vv, bias_ref[p, 0, 0], lo)
        else:
            res = _dil_pair(q, kk, vv, bias_ref[p, 1, 0], lo)
        t0 = r + dil * DIL_N * c
        out_rows = pl.ds(t0, DIL_N) if dil == 1 else pl.ds(t0, DIL_N, stride=dil)
        return out_rows, res

    def chunks(p, dil, us, first):
        results = [chunk(p, dil, u, first) for u in us]
        for out_rows, res in results:
            for dst, val in zip(acc[p], res):
                dst[out_rows, :] = val

    def merge(rs):
        ms = [a[2][rs, :] for a in acc]
        mx = functools.reduce(jnp.maximum, ms)
        fs = [jnp.exp2(m - mx) for m in ms]
        num = sum(f * a[0][rs, :] for f, a in zip(fs, acc))
        den = sum(f * a[1][rs, :] for f, a in zip(fs, acc))
        o_ref[0, rs, :] = (num / den).astype(o_ref.dtype)

    for p, (window, dil) in enumerate(DIL_PATTERNS):
        if dil > 1:
            chunks(p, dil, range(dil), True)
            if seq > window:
                chunks(p, dil, range(dil * (seq // window - 1)), False)
    (p1, (window1, _)), = [(p, pat) for p, pat in enumerate(DIL_PATTERNS) if pat[1] == 1]
    assert window1 == DIL_N
    group = 2
    for g0 in range(0, seq // DIL_N, group):
        if g0 == 0:
            chunks(p1, 1, [0], True)
            chunks(p1, 1, list(range(group - 1)), False)
        else:
            chunks(p1, 1, [c - 1 for c in range(g0, g0 + group)], False)
        merge(slice(g0 * DIL_N, (g0 + group) * DIL_N))


def _dilated_attention(nat, strided, bias):
    b, s, _ = nat.shape
    w = DIL_WIDTH
    npat = len(DIL_PATTERNS)
    tiles = w // LANES
    blk = pl.BlockSpec((1, s, LANES), lambda bi, hp: (bi, 0, hp))
    nat_specs = [pl.BlockSpec((1, s, LANES), lambda bi, hp, sec=sec: (bi, 0, sec * tiles + hp)) for sec in range(3)]
    strided_specs, strided_args = [], []
    for arr, dil in zip(strided, STRIDED_DILATIONS):
        for sec in range(3):
            strided_specs.append(pl.BlockSpec((1, dil, s // dil, LANES),
                                              lambda bi, hp, sec=sec: (bi, 0, 0, sec * tiles + hp)))
            strided_args.append(arr)
    return pl.pallas_call(
        functools.partial(_dilated_kernel, seq=s),
        grid=(b, DIL_HEADS // 2),
        in_specs=nat_specs + strided_specs
                 + [pl.BlockSpec((npat, 2, 1, 2 * DIL_N, 2 * DIL_N), lambda bi, hp: (0, 0, hp, 0, 0))],
        out_specs=blk,
        out_shape=jax.ShapeDtypeStruct((b, s, w), BF16),
        scratch_shapes=[pltpu.VMEM((s, LANES), F32)] * (3 * npat),
        compiler_params=_params("parallel", "parallel"),
        name="dilated",
    )(nat, nat, nat, *strided_args, bias)


def _post_mixer_kernel(x_ref, om_ref, od_ref, wout_ref, gpm_ref, gpx_ref, wxq_ref, kx_ref, vx_ref,
                       wxo_ref, gpox_ref, wup_ref, wdn_ref, out_ref, wup_bf_ref, wdn_bf_ref):
    wup_bf_ref[...] = wup_ref[...].astype(BF16)
    wdn_bf_ref[...] = wdn_ref[...].astype(BF16)
    subs = [slice(c * ROW_SUBTILE, (c + 1) * ROW_SUBTILE) for c in range(x_ref.shape[1] // ROW_SUBTILE)]
    wout_a, wout_b = wout_ref[:MLA_WIDTH, :].astype(BF16), wout_ref[MLA_WIDTH:, :].astype(BF16)
    wxq, wxo = wxq_ref[...].astype(BF16), wxo_ref[...].astype(BF16)
    a = [_dot(om_ref[0, rs], wout_a) + _dot(od_ref[0, rs], wout_b) for rs in subs]
    x1 = [x_ref[0, rs] + _rms(ai, gpm_ref[...]) for rs, ai in zip(subs, a)]
    q = [(_dot(_rms(xi, gpx_ref[...]).astype(BF16), wxq) * (X_HEAD_DIM ** -0.5 * LOG2E)).astype(BF16)
         for xi in x1]
    o = []
    for qi in q:
        heads = []
        for h in range(X_HEADS):
            sl = slice(h * X_HEAD_DIM, (h + 1) * X_HEAD_DIM)
            s = _dot_nt(qi[:, sl], kx_ref[0, :, sl])
            m = jnp.max(s, axis=-1, keepdims=True)
            e = jnp.exp2(s - m)
            den = jnp.sum(e, axis=-1, keepdims=True)
            heads.append((_dot(e.astype(BF16), vx_ref[0, :, sl]) * (1.0 / den)).astype(BF16))
        o.append(jnp.concatenate(heads, axis=-1))
    cx = [_dot(oi, wxo) for oi in o]
    for rs, xi, ci in zip(subs, x1, cx):
        out_ref[0, rs] = xi + _rms(ci, gpox_ref[...])


def _post_mixer(l, x, om, od, wout, gpm, gpx, wxq, kx, vx, wxo, gpox, wup, wdn, tm=1024):
    b, s, d = x.shape
    steps = b * (s // tm)
    tok = lambda w: pl.BlockSpec((1, tm, w), lambda bi, i: (bi, i, 0))
    full = lambda a: _layer_spec(a, l)
    memb = pl.BlockSpec((1, MEM_LEN, d), lambda bi, i: (bi, 0, 0))
    band_in = lambda a: pl.BlockSpec((None, a.shape[1] // steps, a.shape[2]),
                                     lambda bi, i: (l, bi * (s // tm) + i, 0))
    band_out = lambda a: pl.BlockSpec((a.shape[1] // steps, a.shape[2]), lambda bi, i: (bi * (s // tm) + i, 0))
    return pl.pallas_call(
        _post_mixer_kernel,
        grid=(b, s // tm),
        in_specs=[tok(d), tok(MLA_WIDTH), tok(DIL_WIDTH),
                  full(wout), full(gpm), full(gpx), full(wxq), memb, memb, full(wxo), full(gpox),
                  band_in(wup), band_in(wdn)],
        out_specs=[tok(d), band_out(wup), band_out(wdn)],
        out_shape=[jax.ShapeDtypeStruct((b, s, d), F32),
                   jax.ShapeDtypeStruct(wup.shape[1:], BF16), jax.ShapeDtypeStruct(wdn.shape[1:], BF16)],
        compiler_params=_params("parallel", "parallel"),
        name="post_mixer",
    )(x, om, od, wout, gpm, gpx, wxq, kx, vx, wxo, gpox, wup, wdn)


def _mlp_kernel(x_ref, gpre_ref, wup_ref, wdn_ref, gpost_ref, out_ref):
    subs = [slice(c * ROW_SUBTILE, (c + 1) * ROW_SUBTILE) for c in range(x_ref.shape[0] // ROW_SUBTILE)]
    u = [jnp.maximum(_dot(_rms(x_ref[rs, :], gpre_ref[...]).astype(BF16), wup_ref[...]), 0.0) for rs in subs]
    y = [_dot((ui * ui).astype(BF16), wdn_ref[...]) for ui in u]
    for rs, yi in zip(subs, y):
        out_ref[rs, :] = x_ref[rs, :] + _rms(yi, gpost_ref[...])


def _mlp(l, x2d, gpre, wup, wdn, gpost, tm=512):
    n, d = x2d.shape
    row = pl.BlockSpec((tm, d), lambda i: (i, 0))
    full = lambda a: _layer_spec(a, l)
    whole = lambda a: pl.BlockSpec(a.shape, lambda i: (0, 0), pipeline_mode=pl.Buffered(1))
    return pl.pallas_call(
        _mlp_kernel,
        grid=(n // tm,),
        in_specs=[row, full(gpre), whole(wup), whole(wdn), full(gpost)],
        out_specs=row,
        out_shape=jax.ShapeDtypeStruct((n, d), F32),
        compiler_params=_params("parallel"),
        name="mlp",
    )(x2d, gpre, wup, wdn, gpost)


def kernel(x, mem, positions, rel_bias, w_in, q_norm_g, w_uq, kv_norm_g, w_ukv, w_mix_out, pre_mix_g, post_mix_g, w_xq, w_xk, w_xv, w_xo, mem_g, pre_xattn_g, post_xattn_g, w_up, w_down, pre_mlp_g, post_mlp_g):
    b, s, d = x.shape
    n = b * s
    ct, st = _rope_tables(positions)
    bias = _bias_tables(rel_bias).reshape(len(DIL_PATTERNS), 2, DIL_HEADS // 2, 2 * DIL_N, 2 * DIL_N)
    gain = lambda a: a.reshape(a.shape[0], 1, -1).astype(F32)
    gains = {k: gain(v) for k, v in dict(
        pre_mix=pre_mix_g, q_norm=q_norm_g, kv_norm=kv_norm_g, post_mix=post_mix_g, mem=mem_g,
        pre_x=pre_xattn_g, post_x=post_xattn_g, pre_mlp=pre_mlp_g, post_mlp=post_mlp_g).items()}
    wlat, wdil, wuq, wukv = _prep_mixer_weights(w_in, w_uq, w_ukv)
    for l in range(DEPTH):
        qkv, nat, *strided, kx, vx = _mixer_in(l, x.reshape(n, d), s, gains["pre_mix"], wlat, wdil, gains["q_norm"],
                                               wuq, gains["kv_norm"], wukv, ct, st, mem, gains["mem"], w_xk, w_xv)
        om = _mla_attention(qkv.reshape(b, s, -1))
        od = _dilated_attention(nat.reshape(b, s, -1), strided, bias)
        x, wup, wdn = _post_mixer(l, x, om, od, w_mix_out, gains["post_mix"], gains["pre_x"], w_xq, kx, vx, w_xo,
                                  gains["post_x"], w_up, w_down)
        x = _mlp(l, x.reshape(n, d), gains["pre_mlp"], wup, wdn, gains["post_mlp"]).reshape(b, s, d)
    return x
```

```python
import functools
import math

import numpy as np
import jax
import jax.numpy as jnp
from jax import lax
from jax.experimental import pallas as pl
from jax.experimental.pallas import tpu as pltpu

D_MODEL = 1024
DEPTH = 2
MEM_LEN = 256
MLA_HEADS = 8
MLA_NOPE_DIM = 64
MLA_ROPE_DIM = 32
MLA_V_DIM = 64
Q_LORA_RANK = 384
KV_LORA_RANK = 256
ROPE_THETA = 10000.0
DIL_HEADS = 8
DIL_HEAD_DIM = 64
DIL_PATTERNS = ((128, 1), (512, 4), (2048, 16))
N_BUCKETS = 32
MAX_DISTANCE = 2048
X_HEADS = 4
X_HEAD_DIM = D_MODEL // X_HEADS
D_FF = 4 * D_MODEL
EPS = 1e-6

MLA_QK_DIM = MLA_NOPE_DIM + MLA_ROPE_DIM
DIL_WIDTH = DIL_HEADS * DIL_HEAD_DIM
MLA_WIDTH = MLA_HEADS * MLA_V_DIM
ROPE_HALF = MLA_ROPE_DIM // 2
DIL_N = 128
LANES = 128
HALF = LANES // 2
ROW_SUBTILE = 256
NEG_BIG = -1e30
LOG2E = math.log2(math.e)
VMEM_LIMIT_BYTES = 56 * 1024 * 1024

BF16 = jnp.bfloat16
F32 = jnp.float32


def _rms(x, g):
    return x * lax.rsqrt(jnp.mean(x * x, axis=-1, keepdims=True) + EPS) * g


def _dot(a, b):
    return jnp.dot(a, b, preferred_element_type=F32)


def _dot_nt(a, b):
    return lax.dot_general(a, b, (((1,), (1,)), ((), ())), preferred_element_type=F32)


def _params(*sem):
    return pltpu.CompilerParams(dimension_semantics=sem, vmem_limit_bytes=VMEM_LIMIT_BYTES)


def _layer_spec(a, l):
    return pl.BlockSpec((None,) + a.shape[1:], lambda *_: (l,) + (0,) * (a.ndim - 1),
                        pipeline_mode=pl.Buffered(1))


def _low_half():
    return lax.broadcasted_iota(jnp.int32, (1, LANES), 1) < HALF


TOKENS_PER_ROW = LANES // ROPE_HALF


def _rope_table_kernel(pos_ref, inv_ref, ct_ref, st_ref):
    tr = pos_ref.shape[0]
    ang = pos_ref[...] * inv_ref[...]
    cos = jnp.cos(ang)
    sin = jnp.sin(ang)
    lane = lax.broadcasted_iota(jnp.int32, (1, LANES), 1)
    x1 = jnp.logical_and(lane >= MLA_NOPE_DIM, lane < MLA_NOPE_DIM + ROPE_HALF)
    x2 = jnp.logical_and(lane >= MLA_NOPE_DIM + ROPE_HALF, lane < MLA_QK_DIM)
    for j in range(TOKENS_PER_ROW):
        to_x1 = (MLA_NOPE_DIM - ROPE_HALF * j) % LANES
        to_x2 = (MLA_NOPE_DIM + ROPE_HALF - ROPE_HALF * j) % LANES
        c1, c2 = pltpu.roll(cos, to_x1, axis=1), pltpu.roll(cos, to_x2, axis=1)
        s1, s2 = pltpu.roll(sin, to_x1, axis=1), pltpu.roll(sin, to_x2, axis=1)
        rows = pl.ds(j, tr, stride=TOKENS_PER_ROW)
        ct_ref[rows, :] = jnp.where(lane < MLA_NOPE_DIM, 1.0, jnp.where(x1, c1, jnp.where(x2, c2, 0.0)))
        st_ref[rows, :] = jnp.where(x1, -s1, jnp.where(x2, s2, 0.0))


def _rope_tables(positions):
    n = positions.size
    rows = n // TOKENS_PER_ROW
    pos_rep = jnp.repeat(positions.reshape(rows, TOKENS_PER_ROW).astype(F32), ROPE_HALF, axis=1)
    inv = ROPE_THETA ** (-jnp.arange(ROPE_HALF, dtype=F32) / ROPE_HALF)
    inv_rep = jnp.tile(inv, TOKENS_PER_ROW).reshape(1, LANES)
    tr = 256
    return pl.pallas_call(
        _rope_table_kernel,
        grid=(rows // tr,),
        in_specs=[pl.BlockSpec((tr, LANES), lambda i: (i, 0)),
                  pl.BlockSpec((1, LANES), lambda i: (0, 0))],
        out_specs=[pl.BlockSpec((tr * TOKENS_PER_ROW, LANES), lambda i: (i, 0))] * 2,
        out_shape=[jax.ShapeDtypeStruct((n, LANES), F32)] * 2,
        compiler_params=_params("parallel"),
        name="rope_tables",
    )(pos_rep, inv_rep)


def _bucket_tables():
    i = np.arange(DIL_N)[:, None]
    m = np.arange(2 * DIL_N)[None, :]
    rel = DIL_N + i - m
    band = (rel >= 0) & (rel <= DIL_N)
    max_exact = N_BUCKETS // 2
    out = []
    for _, dil in DIL_PATTERNS:
        d = np.maximum(rel * dil, 0)
        df = np.maximum(d.astype(np.float64), 1.0)
        large = max_exact + (np.log(df / max_exact) / math.log(MAX_DISTANCE / max_exact)
                             * (N_BUCKETS - max_exact)).astype(np.int32)
        large = np.minimum(large, N_BUCKETS - 1)
        bucket = np.where(d < max_exact, d, large)
        out.append(np.where(band, bucket, -1).astype(np.int32))
    return np.stack(out)


def _bias_table_kernel(rb_ref, bucket_ref, out_ref):
    bucket = bucket_ref[0]
    masked = jnp.full((DIL_N, DIL_N), NEG_BIG, F32)
    for h in range(DIL_HEADS):
        tab = jnp.full(bucket.shape, NEG_BIG, F32)
        for b in range(N_BUCKETS):
            tab = jnp.where(bucket == b, rb_ref[b, h] * LOG2E, tab)
        out_ref[0, 0, h] = jnp.concatenate([tab[:, DIL_N:], masked], axis=1)
        out_ref[0, 1, h] = tab


def _bias_tables(rel_bias):
    buckets = jnp.asarray(_bucket_tables())
    npat = len(DIL_PATTERNS)
    return pl.pallas_call(
        _bias_table_kernel,
        grid=(npat,),
        in_specs=[pl.BlockSpec(memory_space=pltpu.SMEM),
                  pl.BlockSpec((1, DIL_N, 2 * DIL_N), lambda p: (p, 0, 0))],
        out_specs=pl.BlockSpec((1, 2, DIL_HEADS, DIL_N, 2 * DIL_N), lambda p: (p, 0, 0, 0, 0)),
        out_shape=jax.ShapeDtypeStruct((npat, 2, DIL_HEADS, DIL_N, 2 * DIL_N), F32),
        compiler_params=_params("parallel"),
        name="bias_tables",
    )(rel_bias.astype(F32), buckets)


C_KR = Q_LORA_RANK + KV_LORA_RANK
C_QD = C_KR + LANES
QPAD = MLA_HEADS * LANES
DIL_SLABS = 3 * DIL_WIDTH // LANES
STRIDED_DILATIONS = tuple(d for _, d in DIL_PATTERNS if d > 1)


def _rotary(a, ct, st, x1_lanes):
    partner = jnp.where(x1_lanes, pltpu.roll(a, LANES - ROPE_HALF, axis=1), pltpu.roll(a, ROPE_HALF, axis=1))
    return a * ct + partner * st


def _mixer_in_kernel(x_ref, gpre_ref, wlat_ref, wdil_ref, gq_ref, wuq_ref, gkv_ref, wukv_ref, ct_ref, st_ref,
                     mem_ref, gmem_ref, wxk_ref, wxv_ref,
                     qkv_ref, nat_ref, d4_ref, d16_ref, kx_ref, vx_ref, slab_scr, cls_scr):
    lane = lax.broadcasted_iota(jnp.int32, (1, LANES), 1)
    x1_lanes = jnp.logical_and(lane >= MLA_NOPE_DIM, lane < MLA_NOPE_DIM + ROPE_HALF)
    nsec = DIL_SLABS // 3

    subs = [slice(c * ROW_SUBTILE, (c + 1) * ROW_SUBTILE) for c in range(x_ref.shape[0] // ROW_SUBTILE)]
    wl = wlat_ref[...]
    zeros = lambda n: jnp.zeros((wl.shape[0], n), BF16)
    w_lat = jnp.concatenate([wl[:, :C_KR], zeros(MLA_NOPE_DIM), wl[:, C_KR:], zeros(LANES - MLA_QK_DIM)], axis=1)
    xn = [_rms(x_ref[rs, :], gpre_ref[...]).astype(BF16) for rs in subs]
    zls = [_dot(xi, w_lat) for xi in xn]
    zds = [_dot(xi, wdil_ref[...]) for xi in xn]
    for c, (rs, z, zd) in enumerate(zip(subs, zls, zds)):
        ct = ct_ref[rs, :]
        st = st_ref[rs, :]

        per4, per16 = ROW_SUBTILE // 4, ROW_SUBTILE // 16
        for t2 in range(DIL_SLABS // 2):
            vals = []
            for t in (2 * t2, 2 * t2 + 1):
                val = zd[:, t * LANES:(t + 1) * LANES]
                if t < nsec:
                    val = val * (DIL_HEAD_DIM ** -0.5 * LOG2E)
                nat_ref[rs, t * LANES:(t + 1) * LANES] = val.astype(BF16)
                vals.append(val)
            slab_scr[c, t2] = pltpu.pack_elementwise(vals, packed_dtype=BF16)

            def emit(dref, r, rows, packed):
                for i in range(2):
                    tile = pltpu.unpack_elementwise(packed, index=i, packed_dtype=BF16, unpacked_dtype=F32)
                    dref[0, r, rows, (2 * t2 + i) * LANES:(2 * t2 + i + 1) * LANES] = tile.astype(BF16)

            for r4 in range(4):
                cls = slab_scr[c, t2, pl.ds(r4, per4, stride=4), :]
                cls_scr[c, t2, r4] = cls
                emit(d4_ref, r4, slice(c * per4, (c + 1) * per4), cls)
            for r4 in range(4):
                for hi in range(4):
                    emit(d16_ref, r4 + 4 * hi, slice(c * per16, (c + 1) * per16),
                         cls_scr[c, t2, r4, pl.ds(hi, per16, stride=4), :])

        cqn = _rms(z[:, :Q_LORA_RANK], gq_ref[...])
        qa = _dot(cqn.astype(BF16), wuq_ref[...])
        scale = MLA_QK_DIM ** -0.5 * LOG2E
        for h in range(MLA_HEADS):
            hs = slice(h * LANES, (h + 1) * LANES)
            qkv_ref[rs, hs] = (_rotary(qa[:, hs], ct, st, x1_lanes) * scale).astype(BF16)

        ckvn = _rms(z[:, Q_LORA_RANK:C_KR], gkv_ref[...])
        kv = _dot(ckvn.astype(BF16), wukv_ref[...])
        qkv_ref[rs, 2 * QPAD:] = kv[:, QPAD:].astype(BF16)
        kpe = _rotary(z[:, C_KR:C_QD], ct, st, x1_lanes)
        for h in range(MLA_HEADS):
            hs = slice(h * LANES, (h + 1) * LANES)
            qkv_ref[rs, QPAD + h * LANES:QPAD + (h + 1) * LANES] = (kv[:, hs] + kpe).astype(BF16)

    mn = _rms(mem_ref[0], gmem_ref[...]).astype(BF16)
    kx_ref[0] = _dot(mn, wxk_ref[...].astype(BF16)).astype(BF16)
    vx_ref[0] = _dot(mn, wxv_ref[...].astype(BF16)).astype(BF16)


def _prep_mixer_weights(w_in, w_uq, w_ukv):
    depth = w_in.shape[0]
    w_uq, w_ukv = w_uq.astype(BF16), w_ukv.astype(BF16)
    o3 = Q_LORA_RANK + KV_LORA_RANK + MLA_ROPE_DIM
    wlat, wdil = w_in[..., :o3].astype(BF16), w_in[..., o3:].astype(BF16)

    wq = w_uq.reshape(depth, Q_LORA_RANK, MLA_HEADS, MLA_QK_DIM)
    zq = jnp.zeros((depth, Q_LORA_RANK, MLA_HEADS, 32), BF16)
    wuq = jnp.concatenate([wq, zq], axis=-1).reshape(depth, Q_LORA_RANK, QPAD)

    wkv = w_ukv.reshape(depth, KV_LORA_RANK, MLA_HEADS, MLA_NOPE_DIM + MLA_V_DIM)
    wk = jnp.concatenate([wkv[..., :MLA_NOPE_DIM], jnp.zeros_like(wkv[..., :MLA_NOPE_DIM])], axis=-1)
    wv = wkv[..., MLA_NOPE_DIM:]
    wukv = jnp.concatenate([wk.reshape(depth, KV_LORA_RANK, QPAD), wv.reshape(depth, KV_LORA_RANK, MLA_WIDTH)],
                           axis=-1)
    return wlat, wdil, wuq, wukv


def _mixer_in(l, x2d, seq, gpre, wlat, wdil, gq, wuq, gkv, wukv, ct, st, mem, gmem, wxk, wxv, tm=512):
    n = x2d.shape[0]
    b = n // seq
    tpb = seq // tm
    row = lambda w: pl.BlockSpec((tm, w), lambda i: (i, 0))
    full = lambda a: _layer_spec(a, l)
    outs = [(2 * QPAD + MLA_WIDTH, BF16), (3 * DIL_WIDTH, BF16)]
    memb = pl.BlockSpec((1,) + mem.shape[1:], lambda i: (i // tpb, 0, 0))
    band = mem.shape[2] // tpb
    wband = lambda a: pl.BlockSpec((None, a.shape[1], band), lambda i: (l, 0, i % tpb))
    kvband = pl.BlockSpec((1, mem.shape[1], band), lambda i: (i // tpb, 0, i % tpb))
    strided_specs = [pl.BlockSpec((1, dil, tm // dil, 3 * DIL_WIDTH), lambda i: (i // tpb, 0, i % tpb, 0))
                     for dil in STRIDED_DILATIONS]
    strided_shapes = [jax.ShapeDtypeStruct((b, dil, seq // dil, 3 * DIL_WIDTH), BF16) for dil in STRIDED_DILATIONS]
    return pl.pallas_call(
        _mixer_in_kernel,
        grid=(n // tm,),
        in_specs=[row(D_MODEL), full(gpre), full(wlat), full(wdil), full(gq), full(wuq), full(gkv), full(wukv),
                  row(LANES), row(LANES), memb, full(gmem), wband(wxk), wband(wxv)],
        out_specs=[row(w) for w, _ in outs] + strided_specs + [kvband, kvband],
        out_shape=([jax.ShapeDtypeStruct((n, w), dt) for w, dt in outs] + strided_shapes
                   + [jax.ShapeDtypeStruct(mem.shape, BF16)] * 2),
        scratch_shapes=[pltpu.VMEM((tm // ROW_SUBTILE, DIL_SLABS // 2, ROW_SUBTILE, LANES), jnp.uint32),
                        pltpu.VMEM((tm // ROW_SUBTILE, DIL_SLABS // 2, 4, ROW_SUBTILE // 4, LANES), jnp.uint32)],
        compiler_params=_params("parallel"),
        name="mixer_in",
    )(x2d, gpre, wlat, wdil, gq, wuq, gkv, wukv, ct, st, mem, gmem, wxk, wxv)


MLA_HEADS_PER_STEP = 4


def _mla_attn_kernel(q_ref, k_ref, v_ref, o_ref, s_scr, m_scr, r_scr, acc_scr, oe_scr, *, tq, nq, nh):
    lo = _low_half()

    def rows(i):
        return slice(i * tq, (i + 1) * tq)

    def block(qi, j):
        return qi * (qi + 1) // 2 + j

    def pieces(qi, j):
        if j < qi:
            return [(0, tq, tq)]
        return [(0, tq // 2, tq // 2), (tq // 2, tq // 2, tq)]

    def scores(h, qi, j):
        slot = h % 2
        hs = slice(h * LANES, (h + 1) * LANES)
        for r0, nr, nk in pieces(qi, j):
            rs = slice(r0, r0 + nr)
            s = _dot_nt(q_ref[0, qi * tq + r0:qi * tq + r0 + nr, hs], k_ref[0, j * tq:j * tq + nk, hs])
            if j == qi:
                row = lax.broadcasted_iota(jnp.int32, (nr, nk), 0) + r0
                col = lax.broadcasted_iota(jnp.int32, (nr, nk), 1)
                s = jnp.where(col <= row, s, NEG_BIG)
            s_scr[slot, block(qi, j), rs, :nk] = s
            m = s[:, :LANES] if j == 0 else jnp.maximum(m_scr[slot, qi, rs], s[:, :LANES])
            for t in range(1, nk // LANES):
                m = jnp.maximum(m, s[:, t * LANES:(t + 1) * LANES])
            if j == qi:
                r_scr[slot, qi, rs] = jnp.broadcast_to(jnp.max(m, axis=-1, keepdims=True), (nr, LANES))
            else:
                m_scr[slot, qi, rs] = m

    def weigh(h, qi, j):
        slot = h % 2
        pr = slice((h // 2) * LANES, (h // 2 + 1) * LANES)
        for r0, nr, nk in pieces(qi, j):
            rs = slice(r0, r0 + nr)
            vblk = v_ref[0, j * tq:j * tq + nk, pr]
            s = s_scr[slot, block(qi, j), rs, :nk]
            rmax = r_scr[slot, qi, rs]
            p = jnp.concatenate([jnp.exp2(s[:, t * LANES:(t + 1) * LANES] - rmax) for t in range(nk // LANES)],
                                axis=1)
            pv = _dot(p.astype(BF16), jnp.concatenate([vblk, jnp.ones_like(vblk)], axis=1))
            if j == 0:
                acc_scr[slot, qi, rs] = pv
            else:
                acc_scr[slot, qi, rs] += pv
        if j == qi:
            acc = acc_scr[slot, qi]
            o = acc[:, :LANES] / acc[:, LANES:]
            if h % 2 == 0:
                oe_scr[qi] = o
            else:
                o_ref[0, rows(qi), pr] = jnp.where(lo, oe_scr[qi], o).astype(o_ref.dtype)

    for stage in range(nh + 1):
        for qi in range(nq):
            for j in range(qi + 1):
                if stage < nh:
                    scores(stage, qi, j)
                if stage > 0:
                    weigh(stage - 1, qi, j)


def _mla_attention(qkv, tq=512, nh=MLA_HEADS_PER_STEP):
    b, s, _ = qkv.shape
    nq = s // tq
    groups = MLA_HEADS // nh
    seq = lambda w, first: pl.BlockSpec((1, s, w), lambda bi, hg: (bi, 0, first + hg))
    return pl.pallas_call(
        functools.partial(_mla_attn_kernel, tq=tq, nq=nq, nh=nh),
        grid=(b, MLA_HEADS // nh),
        in_specs=[seq(nh * LANES, 0), seq(nh * LANES, groups), seq(nh * HALF, 2 * QPAD // (nh * HALF))],
        out_specs=seq(nh * HALF, 0),
        out_shape=jax.ShapeDtypeStruct((b, s, MLA_WIDTH), BF16),
        scratch_shapes=[pltpu.VMEM((2, nq * (nq + 1) // 2, tq, tq), F32),
                        pltpu.VMEM((2, nq, tq, LANES), F32),
                        pltpu.VMEM((2, nq, tq, LANES), F32),
                        pltpu.VMEM((2, nq, tq, 2 * LANES), F32),
                        pltpu.VMEM((nq, tq, LANES), F32)],
        compiler_params=_params("parallel", "parallel"),
        name="mla_attn",
    )(qkv, qkv, qkv)


def _dil_pair(q, kk, vv, bias2, lo):
    zero = jnp.zeros_like(q)
    q2 = jnp.concatenate([jnp.where(lo, q, zero), jnp.where(lo, zero, q)], axis=0)
    s = _dot_nt(q2, kk) + bias2
    m = jnp.max(s, axis=-1, keepdims=True)
    e = jnp.exp2(s - m).astype(BF16)
    r = _dot(e, jnp.concatenate([vv, jnp.ones_like(vv)], axis=1))
    mb = jnp.broadcast_to(m, (2 * DIL_N, LANES))
    pick = lambda a: jnp.where(lo, a[:DIL_N], a[DIL_N:])
    return pick(r[:, :LANES]), pick(r[:, LANES:]), pick(mb)


def _dil_pair_first(q, kk, vv, bias2, lo):
    zk, zv = jnp.zeros_like(kk), jnp.zeros_like(vv)
    ones_lo = jnp.broadcast_to(jnp.where(lo, 1.0, 0.0), vv.shape).astype(BF16)
    ones_hi = jnp.broadcast_to(jnp.where(lo, 0.0, 1.0), vv.shape).astype(BF16)
    keys2 = jnp.concatenate([jnp.where(lo, kk, zk), jnp.where(lo, zk, kk)], axis=0)
    s = _dot_nt(q, keys2) + jnp.concatenate([bias2[:DIL_N, :DIL_N], bias2[DIL_N:, :DIL_N]], axis=1)
    m0 = jnp.max(s[:, :DIL_N], axis=-1, keepdims=True)
    m1 = jnp.max(s[:, DIL_N:], axis=-1, keepdims=True)
    e = jnp.concatenate([jnp.exp2(s[:, :DIL_N] - m0), jnp.exp2(s[:, DIL_N:] - m1)], axis=1).astype(BF16)
    w = jnp.concatenate([jnp.concatenate([jnp.where(lo, vv, zv), ones_lo], axis=1),
                         jnp.concatenate([jnp.where(lo, zv, vv), ones_hi], axis=1)], axis=0)
    r = _dot(e, w)
    return r[:, :LANES], r[:, LANES:], jnp.where(lo, m0, m1)


def _dilated_kernel(q_ref, k_ref, v_ref, q4_ref, k4_ref, v4_ref, q16_ref, k16_ref, v16_ref, bias_ref, o_ref,
                    *acc_scr, seq):
    lo = _low_half()
    acc = [acc_scr[3 * p:3 * p + 3] for p in range(len(DIL_PATTERNS))]
    strided = {4: (q4_ref, k4_ref, v4_ref), 16: (q16_ref, k16_ref, v16_ref)}

    def chunk(p, dil, u, first):
        r, c = (u, 0) if first else (u % dil, 1 + u // dil)
        row_q = c * DIL_N
        row_k, nk = (row_q, DIL_N) if first else (row_q - DIL_N, 2 * DIL_N)
        if dil == 1:
            q = q_ref[0, pl.ds(row_q, DIL_N), :]
            kk = k_ref[0, pl.ds(row_k, nk), :]
            vv = v_ref[0, pl.ds(row_k, nk), :]
        else:
            qr, kr, vr = strided[dil]
            q = qr[0, r, pl.ds(row_q, DIL_N), :]
            kk = kr[0, r, pl.ds(row_k, nk), :]
            vv = vr[0, r, pl.ds(row_k, nk), :]
        if first:
            res = _dil_pair_first(q, kk, vv, bias_ref[p, 0, 0], lo)
        else:
            res = _dil_pair(q, kk, vv, bias_ref[p, 1, 0], lo)
        t0 = r + dil * DIL_N * c
        out_rows = pl.ds(t0, DIL_N) if dil == 1 else pl.ds(t0, DIL_N, stride=dil)
        return out_rows, res

    def chunks(p, dil, us, first):
        results = [chunk(p, dil, u, first) for u in us]
        for out_rows, res in results:
            for dst, val in zip(acc[p], res):
                dst[out_rows, :] = val

    def merge(rs):
        ms = [a[2][rs, :] for a in acc]
        mx = functools.reduce(jnp.maximum, ms)
        fs = [jnp.exp2(m - mx) for m in ms]
        num = sum(f * a[0][rs, :] for f, a in zip(fs, acc))
        den = sum(f * a[1][rs, :] for f, a in zip(fs, acc))
        o_ref[0, rs, :] = (num / den).astype(o_ref.dtype)

    for p, (window, dil) in enumerate(DIL_PATTERNS):
        if dil > 1:
            chunks(p, dil, range(dil), True)
            if seq > window:
                chunks(p, dil, range(dil * (seq // window - 1)), False)
    (p1, (window1, _)), = [(p, pat) for p, pat in enumerate(DIL_PATTERNS) if pat[1] == 1]
    assert window1 == DIL_N
    group = 2
    for g0 in range(0, seq // DIL_N, group):
        if g0 == 0:
            chunks(p1, 1, [0], True)
            chunks(p1, 1, list(range(group - 1)), False)
        else:
            chunks(p1, 1, [c - 1 for c in range(g0, g0 + group)], False)
        merge(slice(g0 * DIL_N, (g0 + group) * DIL_N))


def _dilated_attention(nat, strided, bias):
    b, s, _ = nat.shape
    w = DIL_WIDTH
    npat = len(DIL_PATTERNS)
    tiles = w // LANES
    blk = pl.BlockSpec((1, s, LANES), lambda bi, hp: (bi, 0, hp))
    nat_specs = [pl.BlockSpec((1, s, LANES), lambda bi, hp, sec=sec: (bi, 0, sec * tiles + hp)) for sec in range(3)]
    strided_specs, strided_args = [], []
    for arr, dil in zip(strided, STRIDED_DILATIONS):
        for sec in range(3):
            strided_specs.append(pl.BlockSpec((1, dil, s // dil, LANES),
                                              lambda bi, hp, sec=sec: (bi, 0, 0, sec * tiles + hp)))
            strided_args.append(arr)
    return pl.pallas_call(
        functools.partial(_dilated_kernel, seq=s),
        grid=(b, DIL_HEADS // 2),
        in_specs=nat_specs + strided_specs
                 + [pl.BlockSpec((npat, 2, 1, 2 * DIL_N, 2 * DIL_N), lambda bi, hp: (0, 0, hp, 0, 0))],
        out_specs=blk,
        out_shape=jax.ShapeDtypeStruct((b, s, w), BF16),
        scratch_shapes=[pltpu.VMEM((s, LANES), F32)] * (3 * npat),
        compiler_params=_params("parallel", "parallel"),
        name="dilated",
    )(nat, nat, nat, *strided_args, bias)


def _post_mixer_kernel(x_ref, om_ref, od_ref, wout_ref, gpm_ref, gpx_ref, wxq_ref, kx_ref, vx_ref,
                       wxo_ref, gpox_ref, wup_ref, wdn_ref, out_ref, wup_bf_ref, wdn_bf_ref):
    wup_bf_ref[...] = wup_ref[...].astype(BF16)
    wdn_bf_ref[...] = wdn_ref[...].astype(BF16)
    subs = [slice(c * ROW_SUBTILE, (c + 1) * ROW_SUBTILE) for c in range(x_ref.shape[1] // ROW_SUBTILE)]
    wout_a, wout_b = wout_ref[:MLA_WIDTH, :].astype(BF16), wout_ref[MLA_WIDTH:, :].astype(BF16)
    wxq, wxo = wxq_ref[...].astype(BF16), wxo_ref[...].astype(BF16)
    a = [_dot(om_ref[0, rs], wout_a) + _dot(od_ref[0, rs], wout_b) for rs in subs]
    x1 = [x_ref[0, rs] + _rms(ai, gpm_ref[...]) for rs, ai in zip(subs, a)]
    q = [(_dot(_rms(xi, gpx_ref[...]).astype(BF16), wxq) * (X_HEAD_DIM ** -0.5 * LOG2E)).astype(BF16)
         for xi in x1]
    o = []
    for qi in q:
        heads = []
        for h in range(X_HEADS):
            sl = slice(h * X_HEAD_DIM, (h + 1) * X_HEAD_DIM)
            s = _dot_nt(qi[:, sl], kx_ref[0, :, sl])
            m = jnp.max(s, axis=-1, keepdims=True)
            e = jnp.exp2(s - m)
            den = jnp.sum(e, axis=-1, keepdims=True)
            heads.append((_dot(e.astype(BF16), vx_ref[0, :, sl]) * (1.0 / den)).astype(BF16))
        o.append(jnp.concatenate(heads, axis=-1))
    cx = [_dot(oi, wxo) for oi in o]
    for rs, xi, ci in zip(subs, x1, cx):
        out_ref[0, rs] = xi + _rms(ci, gpox_ref[...])


def _post_mixer(l, x, om, od, wout, gpm, gpx, wxq, kx, vx, wxo, gpox, wup, wdn, tm=1024):
    b, s, d = x.shape
    steps = b * (s // tm)
    tok = lambda w: pl.BlockSpec((1, tm, w), lambda bi, i: (bi, i, 0))
    full = lambda a: _layer_spec(a, l)
    memb = pl.BlockSpec((1, MEM_LEN, d), lambda bi, i: (bi, 0, 0))
    band_in = lambda a: pl.BlockSpec((None, a.shape[1] // steps, a.shape[2]),
                                     lambda bi, i: (l, bi * (s // tm) + i, 0))
    band_out = lambda a: pl.BlockSpec((a.shape[1] // steps, a.shape[2]), lambda bi, i: (bi * (s // tm) + i, 0))
    return pl.pallas_call(
        _post_mixer_kernel,
        grid=(b, s // tm),
        in_specs=[tok(d), tok(MLA_WIDTH), tok(DIL_WIDTH),
                  full(wout), full(gpm), full(gpx), full(wxq), memb, memb, full(wxo), full(gpox),
                  band_in(wup), band_in(wdn)],
        out_specs=[tok(d), band_out(wup), band_out(wdn)],
        out_shape=[jax.ShapeDtypeStruct((b, s, d), F32),
                   jax.ShapeDtypeStruct(wup.shape[1:], BF16), jax.ShapeDtypeStruct(wdn.shape[1:], BF16)],
        compiler_params=_params("parallel", "parallel"),
        name="post_mixer",
    )(x, om, od, wout, gpm, gpx, wxq, kx, vx, wxo, gpox, wup, wdn)


def _mlp_kernel(x_ref, gpre_ref, wup_ref, wdn_ref, gpost_ref, out_ref):
    subs = [slice(c * ROW_SUBTILE, (c + 1) * ROW_SUBTILE) for c in range(x_ref.shape[0] // ROW_SUBTILE)]
    u = [jnp.maximum(_dot(_rms(x_ref[rs, :], gpre_ref[...]).astype(BF16), wup_ref[...]), 0.0) for rs in subs]
    y = [_dot((ui * ui).astype(BF16), wdn_ref[...]) for ui in u]
    for rs, yi in zip(subs, y):
        out_ref[rs, :] = x_ref[rs, :] + _rms(yi, gpost_ref[...])


def _mlp(l, x2d, gpre, wup, wdn, gpost, tm=512):
    n, d = x2d.shape
    row = pl.BlockSpec((tm, d), lambda i: (i, 0))
    full = lambda a: _layer_spec(a, l)
    whole = lambda a: pl.BlockSpec(a.shape, lambda i: (0, 0), pipeline_mode=pl.Buffered(1))
    return pl.pallas_call(
        _mlp_kernel,
        grid=(n // tm,),
        in_specs=[row, full(gpre), whole(wup), whole(wdn), full(gpost)],
        out_specs=row,
        out_shape=jax.ShapeDtypeStruct((n, d), F32),
        compiler_params=_params("parallel"),
        name="mlp",
    )(x2d, gpre, wup, wdn, gpost)


def kernel(x, mem, positions, rel_bias, w_in, q_norm_g, w_uq, kv_norm_g, w_ukv, w_mix_out, pre_mix_g, post_mix_g, w_xq, w_xk, w_xv, w_xo, mem_g, pre_xattn_g, post_xattn_g, w_up, w_down, pre_mlp_g, post_mlp_g):
    b, s, d = x.shape
    n = b * s
    ct, st = _rope_tables(positions)
    bias = _bias_tables(rel_bias).reshape(len(DIL_PATTERNS), 2, DIL_HEADS // 2, 2 * DIL_N, 2 * DIL_N)
    gain = lambda a: a.reshape(a.shape[0], 1, -1).astype(F32)
    gains = {k: gain(v) for k, v in dict(
        pre_mix=pre_mix_g, q_norm=q_norm_g, kv_norm=kv_norm_g, post_mix=post_mix_g, mem=mem_g,
        pre_x=pre_xattn_g, post_x=post_xattn_g, pre_mlp=pre_mlp_g, post_mlp=post_mlp_g).items()}
    wlat, wdil, wuq, wukv = _prep_mixer_weights(w_in, w_uq, w_ukv)
    for l in range(DEPTH):
        qkv, nat, *strided, kx, vx = _mixer_in(l, x.reshape(n, d), s, gains["pre_mix"], wlat, wdil, gains["q_norm"],
                                               wuq, gains["kv_norm"], wukv, ct, st, mem, gains["mem"], w_xk, w_xv)
        om = _mla_attention(qkv.reshape(b, s, -1))
        od = _dilated_attention(nat.reshape(b, s, -1), strided, bias)
        x, wup, wdn = _post_mixer(l, x, om, od, w_mix_out, gains["post_mix"], gains["pre_x"], w_xq, kx, vx, w_xo,
                                  gains["post_x"], w_up, w_down)
        x = _mlp(l, x.reshape(n, d), gains["pre_mlp"], wup, wdn, gains["post_mlp"]).reshape(b, s, d)
    return x
```

```python
import functools
import math

import numpy as np
import jax
import jax.numpy as jnp
from jax import lax
from jax.experimental import pallas as pl
from jax.experimental.pallas import tpu as pltpu

D_MODEL = 1024
DEPTH = 2
MEM_LEN = 256
MLA_HEADS = 8
MLA_NOPE_DIM = 64
MLA_ROPE_DIM = 32
MLA_V_DIM = 64
Q_LORA_RANK = 384
KV_LORA_RANK = 256
ROPE_THETA = 10000.0
DIL_HEADS = 8
DIL_HEAD_DIM = 64
DIL_PATTERNS = ((128, 1), (512, 4), (2048, 16))
N_BUCKETS = 32
MAX_DISTANCE = 2048
X_HEADS = 4
X_HEAD_DIM = D_MODEL // X_HEADS
D_FF = 4 * D_MODEL
EPS = 1e-6

MLA_QK_DIM = MLA_NOPE_DIM + MLA_ROPE_DIM
DIL_WIDTH = DIL_HEADS * DIL_HEAD_DIM
MLA_WIDTH = MLA_HEADS * MLA_V_DIM
ROPE_HALF = MLA_ROPE_DIM // 2
DIL_N = 128
LANES = 128
HALF = LANES // 2
ROW_SUBTILE = 256
NEG_BIG = -1e30
LOG2E = math.log2(math.e)
VMEM_LIMIT_BYTES = 56 * 1024 * 1024

BF16 = jnp.bfloat16
F32 = jnp.float32


def _rms(x, g):
    return x * lax.rsqrt(jnp.mean(x * x, axis=-1, keepdims=True) + EPS) * g


def _dot(a, b):
    return jnp.dot(a, b, preferred_element_type=F32)


def _dot_nt(a, b):
    return lax.dot_general(a, b, (((1,), (1,)), ((), ())), preferred_element_type=F32)


def _params(*sem):
    return pltpu.CompilerParams(dimension_semantics=sem, vmem_limit_bytes=VMEM_LIMIT_BYTES)


def _layer_spec(a, l):
    return pl.BlockSpec((None,) + a.shape[1:], lambda *_: (l,) + (0,) * (a.ndim - 1),
                        pipeline_mode=pl.Buffered(1))


def _low_half():
    return lax.broadcasted_iota(jnp.int32, (1, LANES), 1) < HALF


TOKENS_PER_ROW = LANES // ROPE_HALF


def _rope_table_kernel(pos_ref, inv_ref, ct_ref, st_ref):
    tr = pos_ref.shape[0]
    ang = pos_ref[...] * inv_ref[...]
    cos = jnp.cos(ang)
    sin = jnp.sin(ang)
    lane = lax.broadcasted_iota(jnp.int32, (1, LANES), 1)
    x1 = jnp.logical_and(lane >= MLA_NOPE_DIM, lane < MLA_NOPE_DIM + ROPE_HALF)
    x2 = jnp.logical_and(lane >= MLA_NOPE_DIM + ROPE_HALF, lane < MLA_QK_DIM)
    for j in range(TOKENS_PER_ROW):
        to_x1 = (MLA_NOPE_DIM - ROPE_HALF * j) % LANES
        to_x2 = (MLA_NOPE_DIM + ROPE_HALF - ROPE_HALF * j) % LANES
        c1, c2 = pltpu.roll(cos, to_x1, axis=1), pltpu.roll(cos, to_x2, axis=1)
        s1, s2 = pltpu.roll(sin, to_x1, axis=1), pltpu.roll(sin, to_x2, axis=1)
        rows = pl.ds(j, tr, stride=TOKENS_PER_ROW)
        ct_ref[rows, :] = jnp.where(lane < MLA_NOPE_DIM, 1.0, jnp.where(x1, c1, jnp.where(x2, c2, 0.0)))
        st_ref[rows, :] = jnp.where(x1, -s1, jnp.where(x2, s2, 0.0))


def _rope_tables(positions):
    n = positions.size
    rows = n // TOKENS_PER_ROW
    pos_rep = jnp.repeat(positions.reshape(rows, TOKENS_PER_ROW).astype(F32), ROPE_HALF, axis=1)
    inv = ROPE_THETA ** (-jnp.arange(ROPE_HALF, dtype=F32) / ROPE_HALF)
    inv_rep = jnp.tile(inv, TOKENS_PER_ROW).reshape(1, LANES)
    tr = 256
    return pl.pallas_call(
        _rope_table_kernel,
        grid=(rows // tr,),
        in_specs=[pl.BlockSpec((tr, LANES), lambda i: (i, 0)),
                  pl.BlockSpec((1, LANES), lambda i: (0, 0))],
        out_specs=[pl.BlockSpec((tr * TOKENS_PER_ROW, LANES), lambda i: (i, 0))] * 2,
        out_shape=[jax.ShapeDtypeStruct((n, LANES), F32)] * 2,
        compiler_params=_params("parallel"),
        name="rope_tables",
    )(pos_rep, inv_rep)


def _bucket_tables():
    i = np.arange(DIL_N)[:, None]
    m = np.arange(2 * DIL_N)[None, :]
    rel = DIL_N + i - m
    band = (rel >= 0) & (rel <= DIL_N)
    max_exact = N_BUCKETS // 2
    out = []
    for _, dil in DIL_PATTERNS:
        d = np.maximum(rel * dil, 0)
        df = np.maximum(d.astype(np.float64), 1.0)
        large = max_exact + (np.log(df / max_exact) / math.log(MAX_DISTANCE / max_exact)
                             * (N_BUCKETS - max_exact)).astype(np.int32)
        large = np.minimum(large, N_BUCKETS - 1)
        bucket = np.where(d < max_exact, d, large)
        out.append(np.where(band, bucket, -1).astype(np.int32))
    return np.stack(out)


def _bias_table_kernel(rb_ref, bucket_ref, out_ref):
    bucket = bucket_ref[0]
    masked = jnp.full((DIL_N, DIL_N), NEG_BIG, F32)
    for h in range(DIL_HEADS):
        tab = jnp.full(bucket.shape, NEG_BIG, F32)
        for b in range(N_BUCKETS):
            tab = jnp.where(bucket == b, rb_ref[b, h] * LOG2E, tab)
        out_ref[0, 0, h] = jnp.concatenate([tab[:, DIL_N:], masked], axis=1)
        out_ref[0, 1, h] = tab


def _bias_tables(rel_bias):
    buckets = jnp.asarray(_bucket_tables())
    npat = len(DIL_PATTERNS)
    return pl.pallas_call(
        _bias_table_kernel,
        grid=(npat,),
        in_specs=[pl.BlockSpec(memory_space=pltpu.SMEM),
                  pl.BlockSpec((1, DIL_N, 2 * DIL_N), lambda p: (p, 0, 0))],
        out_specs=pl.BlockSpec((1, 2, DIL_HEADS, DIL_N, 2 * DIL_N), lambda p: (p, 0, 0, 0, 0)),
        out_shape=jax.ShapeDtypeStruct((npat, 2, DIL_HEADS, DIL_N, 2 * DIL_N), F32),
        compiler_params=_params("parallel"),
        name="bias_tables",
    )(rel_bias.astype(F32), buckets)


C_KR = Q_LORA_RANK + KV_LORA_RANK
C_QD = C_KR + LANES
QPAD = MLA_HEADS * LANES
DIL_SLABS = 3 * DIL_WIDTH // LANES
STRIDED_DILATIONS = tuple(d for _, d in DIL_PATTERNS if d > 1)


def _rotary(a, ct, st, x1_lanes):
    partner = jnp.where(x1_lanes, pltpu.roll(a, LANES - ROPE_HALF, axis=1), pltpu.roll(a, ROPE_HALF, axis=1))
    return a * ct + partner * st


def _mixer_in_kernel(x_ref, gpre_ref, wlat_ref, wdil_ref, gq_ref, wuq_ref, gkv_ref, wukv_ref, ct_ref, st_ref,
                     mem_ref, gmem_ref, wxk_ref, wxv_ref,
                     qkv_ref, nat_ref, d4_ref, d16_ref, kx_ref, vx_ref, slab_scr, cls_scr):
    lane = lax.broadcasted_iota(jnp.int32, (1, LANES), 1)
    x1_lanes = jnp.logical_and(lane >= MLA_NOPE_DIM, lane < MLA_NOPE_DIM + ROPE_HALF)
    nsec = DIL_SLABS // 3

    subs = [slice(c * ROW_SUBTILE, (c + 1) * ROW_SUBTILE) for c in range(x_ref.shape[0] // ROW_SUBTILE)]
    wl = wlat_ref[...]
    zeros = lambda n: jnp.zeros((wl.shape[0], n), BF16)
    w_lat = jnp.concatenate([wl[:, :C_KR], zeros(MLA_NOPE_DIM), wl[:, C_KR:], zeros(LANES - MLA_QK_DIM)], axis=1)
    xn = [_rms(x_ref[rs, :], gpre_ref[...]).astype(BF16) for rs in subs]
    zls = [_dot(xi, w_lat) for xi in xn]
    zds = [_dot(xi, wdil_ref[...]) for xi in xn]
    for c, (rs, z, zd) in enumerate(zip(subs, zls, zds)):
        ct = ct_ref[rs, :]
        st = st_ref[rs, :]

        per4, per16 = ROW_SUBTILE // 4, ROW_SUBTILE // 16
        for t2 in range(DIL_SLABS // 2):
            vals = []
            for t in (2 * t2, 2 * t2 + 1):
                val = zd[:, t * LANES:(t + 1) * LANES]
                if t < nsec:
                    val = val * (DIL_HEAD_DIM ** -0.5 * LOG2E)
                nat_ref[rs, t * LANES:(t + 1) * LANES] = val.astype(BF16)
                vals.append(val)
            slab_scr[c, t2] = pltpu.pack_elementwise(vals, packed_dtype=BF16)

            def emit(dref, r, rows, packed):
                for i in range(2):
                    tile = pltpu.unpack_elementwise(packed, index=i, packed_dtype=BF16, unpacked_dtype=F32)
                    dref[0, r, rows, (2 * t2 + i) * LANES:(2 * t2 + i + 1) * LANES] = tile.astype(BF16)

            for r4 in range(4):
                cls = slab_scr[c, t2, pl.ds(r4, per4, stride=4), :]
                cls_scr[c, t2, r4] = cls
                emit(d4_ref, r4, slice(c * per4, (c + 1) * per4), cls)
            for r4 in range(4):
                for hi in range(4):
                    emit(d16_ref, r4 + 4 * hi, slice(c * per16, (c + 1) * per16),
                         cls_scr[c, t2, r4, pl.ds(hi, per16, stride=4), :])

        cqn = _rms(z[:, :Q_LORA_RANK], gq_ref[...])
        qa = _dot(cqn.astype(BF16), wuq_ref[...])
        scale = MLA_QK_DIM ** -0.5 * LOG2E
        for h in range(MLA_HEADS):
            hs = slice(h * LANES, (h + 1) * LANES)
            qkv_ref[rs, hs] = (_rotary(qa[:, hs], ct, st, x1_lanes) * scale).astype(BF16)

        ckvn = _rms(z[:, Q_LORA_RANK:C_KR], gkv_ref[...])
        kv = _dot(ckvn.astype(BF16), wukv_ref[...])
        qkv_ref[rs, 2 * QPAD:] = kv[:, QPAD:].astype(BF16)
        kpe = _rotary(z[:, C_KR:C_QD], ct, st, x1_lanes)
        for h in range(MLA_HEADS):
            hs = slice(h * LANES, (h + 1) * LANES)
            qkv_ref[rs, QPAD + h * LANES:QPAD + (h + 1) * LANES] = (kv[:, hs] + kpe).astype(BF16)

    mn = _rms(mem_ref[0], gmem_ref[...]).astype(BF16)
    kx_ref[0] = _dot(mn, wxk_ref[...].astype(BF16)).astype(BF16)
    vx_ref[0] = _dot(mn, wxv_ref[...].astype(BF16)).astype(BF16)


def _prep_mixer_weights(w_in, w_uq, w_ukv):
    depth = w_in.shape[0]
    w_uq, w_ukv = w_uq.astype(BF16), w_ukv.astype(BF16)
    o3 = Q_LORA_RANK + KV_LORA_RANK + MLA_ROPE_DIM
    wlat, wdil = w_in[..., :o3].astype(BF16), w_in[..., o3:].astype(BF16)

    wq = w_uq.reshape(depth, Q_LORA_RANK, MLA_HEADS, MLA_QK_DIM)
    zq = jnp.zeros((depth, Q_LORA_RANK, MLA_HEADS, 32), BF16)
    wuq = jnp.concatenate([wq, zq], axis=-1).reshape(depth, Q_LORA_RANK, QPAD)

    wkv = w_ukv.reshape(depth, KV_LORA_RANK, MLA_HEADS, MLA_NOPE_DIM + MLA_V_DIM)
    wk = jnp.concatenate([wkv[..., :MLA_NOPE_DIM], jnp.zeros_like(wkv[..., :MLA_NOPE_DIM])], axis=-1)
    wv = wkv[..., MLA_NOPE_DIM:]
    wukv = jnp.concatenate([wk.reshape(depth, KV_LORA_RANK, QPAD), wv.reshape(depth, KV_LORA_RANK, MLA_WIDTH)],
                           axis=-1)
    return wlat, wdil, wuq, wukv


def _mixer_in(l, x2d, seq, gpre, wlat, wdil, gq, wuq, gkv, wukv, ct, st, mem, gmem, wxk, wxv, tm=512):
    n = x2d.shape[0]
    b = n // seq
    tpb = seq // tm
    row = lambda w: pl.BlockSpec((tm, w), lambda i: (i, 0))
    full = lambda a: _layer_spec(a, l)
    outs = [(2 * QPAD + MLA_WIDTH, BF16), (3 * DIL_WIDTH, BF16)]
    memb = pl.BlockSpec((1,) + mem.shape[1:], lambda i: (i // tpb, 0, 0))
    band = mem.shape[2] // tpb
    wband = lambda a: pl.BlockSpec((None, a.shape[1], band), lambda i: (l, 0, i % tpb))
    kvband = pl.BlockSpec((1, mem.shape[1], band), lambda i: (i // tpb, 0, i % tpb))
    strided_specs = [pl.BlockSpec((1, dil, tm // dil, 3 * DIL_WIDTH), lambda i: (i // tpb, 0, i % tpb, 0))
                     for dil in STRIDED_DILATIONS]
    strided_shapes = [jax.ShapeDtypeStruct((b, dil, seq // dil, 3 * DIL_WIDTH), BF16) for dil in STRIDED_DILATIONS]
    return pl.pallas_call(
        _mixer_in_kernel,
        grid=(n // tm,),
        in_specs=[row(D_MODEL), full(gpre), full(wlat), full(wdil), full(gq), full(wuq), full(gkv), full(wukv),
                  row(LANES), row(LANES), memb, full(gmem), wband(wxk), wband(wxv)],
        out_specs=[row(w) for w, _ in outs] + strided_specs + [kvband, kvband],
        out_shape=([jax.ShapeDtypeStruct((n, w), dt) for w, dt in outs] + strided_shapes
                   + [jax.ShapeDtypeStruct(mem.shape, BF16)] * 2),
        scratch_shapes=[pltpu.VMEM((tm // ROW_SUBTILE, DIL_SLABS // 2, ROW_SUBTILE, LANES), jnp.uint32),
                        pltpu.VMEM((tm // ROW_SUBTILE, DIL_SLABS // 2, 4, ROW_SUBTILE // 4, LANES), jnp.uint32)],
        compiler_params=_params("parallel"),
        name="mixer_in",
    )(x2d, gpre, wlat, wdil, gq, wuq, gkv, wukv, ct, st, mem, gmem, wxk, wxv)


MLA_HEADS_PER_STEP = 4


def _mla_attn_kernel(q_ref, k_ref, v_ref, o_ref, s_scr, m_scr, r_scr, acc_scr, oe_scr, *, tq, nq, nh):
    lo = _low_half()

    def rows(i):
        return slice(i * tq, (i + 1) * tq)

    def block(qi, j):
        return qi * (qi + 1) // 2 + j

    def pieces(qi, j):
        if j < qi:
            return [(0, tq, tq)]
        return [(0, tq // 2, tq // 2), (tq // 2, tq // 2, tq)]

    def scores(h, qi, j):
        slot = h % 2
        hs = slice(h * LANES, (h + 1) * LANES)
        for r0, nr, nk in pieces(qi, j):
            rs = slice(r0, r0 + nr)
            s = _dot_nt(q_ref[0, qi * tq + r0:qi * tq + r0 + nr, hs], k_ref[0, j * tq:j * tq + nk, hs])
            if j == qi:
                row = lax.broadcasted_iota(jnp.int32, (nr, nk), 0) + r0
                col = lax.broadcasted_iota(jnp.int32, (nr, nk), 1)
                s = jnp.where(col <= row, s, NEG_BIG)
            s_scr[slot, block(qi, j), rs, :nk] = s
            m = s[:, :LANES] if j == 0 else jnp.maximum(m_scr[slot, qi, rs], s[:, :LANES])
            for t in range(1, nk // LANES):
                m = jnp.maximum(m, s[:, t * LANES:(t + 1) * LANES])
            if j == qi:
                r_scr[slot, qi, rs] = jnp.broadcast_to(jnp.max(m, axis=-1, keepdims=True), (nr, LANES))
            else:
                m_scr[slot, qi, rs] = m

    def weigh(h, qi, j):
        slot = h % 2
        pr = slice((h // 2) * LANES, (h // 2 + 1) * LANES)
        for r0, nr, nk in pieces(qi, j):
            rs = slice(r0, r0 + nr)
            vblk = v_ref[0, j * tq:j * tq + nk, pr]
            s = s_scr[slot, block(qi, j), rs, :nk]
            rmax = r_scr[slot, qi, rs]
            p = jnp.concatenate([jnp.exp2(s[:, t * LANES:(t + 1) * LANES] - rmax) for t in range(nk // LANES)],
                                axis=1)
            pv = _dot(p.astype(BF16), jnp.concatenate([vblk, jnp.ones_like(vblk)], axis=1))
            if j == 0:
                acc_scr[slot, qi, rs] = pv
            else:
                acc_scr[slot, qi, rs] += pv
        if j == qi:
            acc = acc_scr[slot, qi]
            o = acc[:, :LANES] / acc[:, LANES:]
            if h % 2 == 0:
                oe_scr[qi] = o
            else:
                o_ref[0, rows(qi), pr] = jnp.where(lo, oe_scr[qi], o).astype(o_ref.dtype)

    for stage in range(nh + 1):
        for qi in range(nq):
            for j in range(qi + 1):
                if stage < nh:
                    scores(stage, qi, j)
                if stage > 0:
                    weigh(stage - 1, qi, j)


def _mla_attention(qkv, tq=512, nh=MLA_HEADS_PER_STEP):
    b, s, _ = qkv.shape
    nq = s // tq
    groups = MLA_HEADS // nh
    seq = lambda w, first: pl.BlockSpec((1, s, w), lambda bi, hg: (bi, 0, first + hg))
    return pl.pallas_call(
        functools.partial(_mla_attn_kernel, tq=tq, nq=nq, nh=nh),
        grid=(b, MLA_HEADS // nh),
        in_specs=[seq(nh * LANES, 0), seq(nh * LANES, groups), seq(nh * HALF, 2 * QPAD // (nh * HALF))],
        out_specs=seq(nh * HALF, 0),
        out_shape=jax.ShapeDtypeStruct((b, s, MLA_WIDTH), BF16),
        scratch_shapes=[pltpu.VMEM((2, nq * (nq + 1) // 2, tq, tq), F32),
                        pltpu.VMEM((2, nq, tq, LANES), F32),
                        pltpu.VMEM((2, nq, tq, LANES), F32),
                        pltpu.VMEM((2, nq, tq, 2 * LANES), F32),
                        pltpu.VMEM((nq, tq, LANES), F32)],
        compiler_params=_params("parallel", "parallel"),
        name="mla_attn",
    )(qkv, qkv, qkv)


def _dil_pair(q, kk, vv, bias2, lo):
    zero = jnp.zeros_like(q)
    q2 = jnp.concatenate([jnp.where(lo, q, zero), jnp.where(lo, zero, q)], axis=0)
    s = _dot_nt(q2, kk) + bias2
    m = jnp.max(s, axis=-1, keepdims=True)
    e = jnp.exp2(s - m).astype(BF16)
    r = _dot(e, jnp.concatenate([vv, jnp.ones_like(vv)], axis=1))
    mb = jnp.broadcast_to(m, (2 * DIL_N, LANES))
    pick = lambda a: jnp.where(lo, a[:DIL_N], a[DIL_N:])
    return pick(r[:, :LANES]), pick(r[:, LANES:]), pick(mb)


def _dil_pair_first(q, kk, vv, bias2, lo):
    zk, zv = jnp.zeros_like(kk), jnp.zeros_like(vv)
    ones_lo = jnp.broadcast_to(jnp.where(lo, 1.0, 0.0), vv.shape).astype(BF16)
    ones_hi = jnp.broadcast_to(jnp.where(lo, 0.0, 1.0), vv.shape).astype(BF16)
    keys2 = jnp.concatenate([jnp.where(lo, kk, zk), jnp.where(lo, zk, kk)], axis=0)
    s = _dot_nt(q, keys2) + jnp.concatenate([bias2[:DIL_N, :DIL_N], bias2[DIL_N:, :DIL_N]], axis=1)
    m0 = jnp.max(s[:, :DIL_N], axis=-1, keepdims=True)
    m1 = jnp.max(s[:, DIL_N:], axis=-1, keepdims=True)
    e = jnp.concatenate([jnp.exp2(s[:, :DIL_N] - m0), jnp.exp2(s[:, DIL_N:] - m1)], axis=1).astype(BF16)
    w = jnp.concatenate([jnp.concatenate([jnp.where(lo, vv, zv), ones_lo], axis=1),
                         jnp.concatenate([jnp.where(lo, zv, vv), ones_hi], axis=1)], axis=0)
    r = _dot(e, w)
    return r[:, :LANES], r[:, LANES:], jnp.where(lo, m0, m1)


def _dilated_kernel(q_ref, k_ref, v_ref, q4_ref, k4_ref, v4_ref, q16_ref, k16_ref, v16_ref, bias_ref, o_ref,
                    *acc_scr, seq):
    lo = _low_half()
    acc = [acc_scr[3 * p:3 * p + 3] for p in range(len(DIL_PATTERNS))]
    strided = {4: (q4_ref, k4_ref, v4_ref), 16: (q16_ref, k16_ref, v16_ref)}

    def chunk(p, dil, u, first):
        r, c = (u, 0) if first else (u % dil, 1 + u // dil)
        row_q = c * DIL_N
        row_k, nk = (row_q, DIL_N) if first else (row_q - DIL_N, 2 * DIL_N)
        if dil == 1:
            q = q_ref[0, pl.ds(row_q, DIL_N), :]
            kk = k_ref[0, pl.ds(row_k, nk), :]
            vv = v_ref[0, pl.ds(row_k, nk), :]
        else:
            qr, kr, vr = strided[dil]
            q = qr[0, r, pl.ds(row_q, DIL_N), :]
            kk = kr[0, r, pl.ds(row_k, nk), :]
            vv = vr[0, r, pl.ds(row_k, nk), :]
        if first:
            res = _dil_pair_first(q, kk, vv, bias_ref[p, 0, 0], lo)
        else:
            res = _dil_pair(q, kk, vv, bias_ref[p, 1, 0], lo)
        t0 = r + dil * DIL_N * c
        out_rows = pl.ds(t0, DIL_N) if dil == 1 else pl.ds(t0, DIL_N, stride=dil)
        return out_rows, res

    def chunks(p, dil, us, first):
        results = [chunk(p, dil, u, first) for u in us]
        for out_rows, res in results:
            for dst, val in zip(acc[p], res):
                dst[out_rows, :] = val

    def merge(rs):
        ms = [a[2][rs, :] for a in acc]
        mx = functools.reduce(jnp.maximum, ms)
        fs = [jnp.exp2(m - mx) for m in ms]
        num = sum(f * a[0][rs, :] for f, a in zip(fs, acc))
        den = sum(f * a[1][rs, :] for f, a in zip(fs, acc))
        o_ref[0, rs, :] = (num / den).astype(o_ref.dtype)

    for p, (window, dil) in enumerate(DIL_PATTERNS):
        if dil > 1:
            chunks(p, dil, range(dil), True)
            if seq > window:
                chunks(p, dil, range(dil * (seq // window - 1)), False)
    (p1, (window1, _)), = [(p, pat) for p, pat in enumerate(DIL_PATTERNS) if pat[1] == 1]
    assert window1 == DIL_N
    group = 2
    for g0 in range(0, seq // DIL_N, group):
        if g0 == 0:
            chunks(p1, 1, [0], True)
            chunks(p1, 1, list(range(group - 1)), False)
        else:
            chunks(p1, 1, [c - 1 for c in range(g0, g0 + group)], False)
        merge(slice(g0 * DIL_N, (g0 + group) * DIL_N))


def _dilated_attention(nat, strided, bias):
    b, s, _ = nat.shape
    w = DIL_WIDTH
    npat = len(DIL_PATTERNS)
    tiles = w // LANES
    blk = pl.BlockSpec((1, s, LANES), lambda hp, bi: (bi, 0, hp))
    nat_specs = [pl.BlockSpec((1, s, LANES), lambda hp, bi, sec=sec: (bi, 0, sec * tiles + hp)) for sec in range(3)]
    strided_specs, strided_args = [], []
    for arr, dil in zip(strided, STRIDED_DILATIONS):
        for sec in range(3):
            strided_specs.append(pl.BlockSpec((1, dil, s // dil, LANES),
                                              lambda hp, bi, sec=sec: (bi, 0, 0, sec * tiles + hp)))
            strided_args.append(arr)
    return pl.pallas_call(
        functools.partial(_dilated_kernel, seq=s),
        grid=(DIL_HEADS // 2, b),
        in_specs=nat_specs + strided_specs
                 + [pl.BlockSpec((npat, 2, 1, 2 * DIL_N, 2 * DIL_N), lambda hp, bi: (0, 0, hp, 0, 0))],
        out_specs=blk,
        out_shape=jax.ShapeDtypeStruct((b, s, w), BF16),
        scratch_shapes=[pltpu.VMEM((s, LANES), F32)] * (3 * npat),
        compiler_params=_params("parallel", "parallel"),
        name="dilated",
    )(nat, nat, nat, *strided_args, bias)


def _post_mixer_kernel(x_ref, om_ref, od_ref, wout_ref, gpm_ref, gpx_ref, wxq_ref, kx_ref, vx_ref,
                       wxo_ref, gpox_ref, wup_ref, wdn_ref, out_ref, wup_bf_ref, wdn_bf_ref):
    wup_bf_ref[...] = wup_ref[...].astype(BF16)
    wdn_bf_ref[...] = wdn_ref[...].astype(BF16)
    subs = [slice(c * ROW_SUBTILE, (c + 1) * ROW_SUBTILE) for c in range(x_ref.shape[1] // ROW_SUBTILE)]
    wout_a, wout_b = wout_ref[:MLA_WIDTH, :].astype(BF16), wout_ref[MLA_WIDTH:, :].astype(BF16)
    wxq, wxo = wxq_ref[...].astype(BF16), wxo_ref[...].astype(BF16)
    a = [_dot(om_ref[0, rs], wout_a) + _dot(od_ref[0, rs], wout_b) for rs in subs]
    x1 = [x_ref[0, rs] + _rms(ai, gpm_ref[...]) for rs, ai in zip(subs, a)]
    q = [(_dot(_rms(xi, gpx_ref[...]).astype(BF16), wxq) * (X_HEAD_DIM ** -0.5 * LOG2E)).astype(BF16)
         for xi in x1]
    o = []
    for qi in q:
        heads = []
        for h in range(X_HEADS):
            sl = slice(h * X_HEAD_DIM, (h + 1) * X_HEAD_DIM)
            s = _dot_nt(qi[:, sl], kx_ref[0, :, sl])
            m = jnp.max(s, axis=-1, keepdims=True)
            e = jnp.exp2(s - m)
            den = jnp.sum(e, axis=-1, keepdims=True)
            heads.append((_dot(e.astype(BF16), vx_ref[0, :, sl]) * (1.0 / den)).astype(BF16))
        o.append(jnp.concatenate(heads, axis=-1))
    cx = [_dot(oi, wxo) for oi in o]
    for rs, xi, ci in zip(subs, x1, cx):
        out_ref[0, rs] = xi + _rms(ci, gpox_ref[...])


def _post_mixer(l, x, om, od, wout, gpm, gpx, wxq, kx, vx, wxo, gpox, wup, wdn, tm=1024):
    b, s, d = x.shape
    steps = b * (s // tm)
    tok = lambda w: pl.BlockSpec((1, tm, w), lambda bi, i: (bi, i, 0))
    full = lambda a: _layer_spec(a, l)
    memb = pl.BlockSpec((1, MEM_LEN, d), lambda bi, i: (bi, 0, 0))
    band_in = lambda a: pl.BlockSpec((None, a.shape[1] // steps, a.shape[2]),
                                     lambda bi, i: (l, bi * (s // tm) + i, 0))
    band_out = lambda a: pl.BlockSpec((a.shape[1] // steps, a.shape[2]), lambda bi, i: (bi * (s // tm) + i, 0))
    return pl.pallas_call(
        _post_mixer_kernel,
        grid=(b, s // tm),
        in_specs=[tok(d), tok(MLA_WIDTH), tok(DIL_WIDTH),
                  full(wout), full(gpm), full(gpx), full(wxq), memb, memb, full(wxo), full(gpox),
                  band_in(wup), band_in(wdn)],
        out_specs=[tok(d), band_out(wup), band_out(wdn)],
        out_shape=[jax.ShapeDtypeStruct((b, s, d), F32),
                   jax.ShapeDtypeStruct(wup.shape[1:], BF16), jax.ShapeDtypeStruct(wdn.shape[1:], BF16)],
        compiler_params=_params("parallel", "parallel"),
        name="post_mixer",
    )(x, om, od, wout, gpm, gpx, wxq, kx, vx, wxo, gpox, wup, wdn)


def _mlp_kernel(x_ref, gpre_ref, wup_ref, wdn_ref, gpost_ref, out_ref):
    subs = [slice(c * ROW_SUBTILE, (c + 1) * ROW_SUBTILE) for c in range(x_ref.shape[0] // ROW_SUBTILE)]
    u = [jnp.maximum(_dot(_rms(x_ref[rs, :], gpre_ref[...]).astype(BF16), wup_ref[...]), 0.0) for rs in subs]
    y = [_dot((ui * ui).astype(BF16), wdn_ref[...]) for ui in u]
    for rs, yi in zip(subs, y):
        out_ref[rs, :] = x_ref[rs, :] + _rms(yi, gpost_ref[...])


def _mlp(l, x2d, gpre, wup, wdn, gpost, tm=512):
    n, d = x2d.shape
    row = pl.BlockSpec((tm, d), lambda i: (i, 0))
    full = lambda a: _layer_spec(a, l)
    whole = lambda a: pl.BlockSpec(a.shape, lambda i: (0, 0), pipeline_mode=pl.Buffered(1))
    return pl.pallas_call(
        _mlp_kernel,
        grid=(n // tm,),
        in_specs=[row, full(gpre), whole(wup), whole(wdn), full(gpost)],
        out_specs=row,
        out_shape=jax.ShapeDtypeStruct((n, d), F32),
        compiler_params=_params("parallel"),
        name="mlp",
    )(x2d, gpre, wup, wdn, gpost)


def kernel(x, mem, positions, rel_bias, w_in, q_norm_g, w_uq, kv_norm_g, w_ukv, w_mix_out, pre_mix_g, post_mix_g, w_xq, w_xk, w_xv, w_xo, mem_g, pre_xattn_g, post_xattn_g, w_up, w_down, pre_mlp_g, post_mlp_g):
    b, s, d = x.shape
    n = b * s
    ct, st = _rope_tables(positions)
    bias = _bias_tables(rel_bias).reshape(len(DIL_PATTERNS), 2, DIL_HEADS // 2, 2 * DIL_N, 2 * DIL_N)
    gain = lambda a: a.reshape(a.shape[0], 1, -1).astype(F32)
    gains = {k: gain(v) for k, v in dict(
        pre_mix=pre_mix_g, q_norm=q_norm_g, kv_norm=kv_norm_g, post_mix=post_mix_g, mem=mem_g,
        pre_x=pre_xattn_g, post_x=post_xattn_g, pre_mlp=pre_mlp_g, post_mlp=post_mlp_g).items()}
    wlat, wdil, wuq, wukv = _prep_mixer_weights(w_in, w_uq, w_ukv)
    for l in range(DEPTH):
        qkv, nat, *strided, kx, vx = _mixer_in(l, x.reshape(n, d), s, gains["pre_mix"], wlat, wdil, gains["q_norm"],
                                               wuq, gains["kv_norm"], wukv, ct, st, mem, gains["mem"], w_xk, w_xv)
        om = _mla_attention(qkv.reshape(b, s, -1))
        od = _dilated_attention(nat.reshape(b, s, -1), strided, bias)
        x, wup, wdn = _post_mixer(l, x, om, od, w_mix_out, gains["post_mix"], gains["pre_x"], w_xq, kx, vx, w_xo,
                                  gains["post_x"], w_up, w_down)
        x = _mlp(l, x.reshape(n, d), gains["pre_mlp"], wup, wdn, gains["post_mlp"]).reshape(b, s, d)
    return x
```
